```python
import math
import jax, jax.numpy as jnp
from jax import lax
import numpy as np

D_MODEL = 1024
BATCH = 32
SEQ = 256
DEPTH = 2
DEC_BATCH = 8
DEC_SEQ = 4096
PAST_LEN = 512

GRID_W = 64
N_BRANCH = 4
BRANCH_W = 256
CONV_W = 3
GQA_HEADS = 4
GQA_KV_HEADS = 2
GQA_GROUP = GQA_HEADS // GQA_KV_HEADS
GQA_HEAD_DIM = 64
RET_HEADS = 4
RET_KEY_DIM = 64
RET_VAL_DIM = 64
RET_CHUNK = 128
DIFF_HEADS = 4
DIFF_HEAD_DIM = 32
Q_BLOCK = 128
ROPE_THETA = 10000.0
EPS = 1e-6
GQA_Q_W = GQA_HEADS * GQA_HEAD_DIM
GQA_KV_W = GQA_KV_HEADS * GQA_HEAD_DIM
RET_QK_W = RET_HEADS * RET_KEY_DIM
RET_V_W = RET_HEADS * RET_VAL_DIM
DIFF_QK_W = DIFF_HEADS * 2 * DIFF_HEAD_DIM
DIFF_V_W = DIFF_HEADS * 2 * DIFF_HEAD_DIM
IN_WIDTHS = (
    BRANCH_W, BRANCH_W, BRANCH_W, BRANCH_W,
    GQA_Q_W, GQA_KV_W, GQA_KV_W, BRANCH_W,
    RET_QK_W, RET_QK_W, RET_V_W, BRANCH_W,
    DIFF_QK_W, DIFF_QK_W, DIFF_V_W, BRANCH_W,
)
IN_W = sum(IN_WIDTHS)

kernel_name = "hybrid_prefix_diffusion_step"


def split_points():
    return [int(s) for s in np.cumsum(IN_WIDTHS)[:-1]]


def rms_norm(x, gain=None):
    xf = x.astype(jnp.float32)
    y = xf * lax.rsqrt(jnp.mean(xf * xf, axis=-1, keepdims=True) + EPS)
    if gain is not None:
        y = y * gain.astype(jnp.float32)
    return y.astype(x.dtype)


def axial_rope_tables(n_tokens, head_dim):
    rows = n_tokens // GRID_W
    row = jnp.repeat(jnp.arange(rows, dtype=jnp.float32), GRID_W)
    col = jnp.tile(jnp.arange(GRID_W, dtype=jnp.float32), rows)
    n_axis = head_dim // 4
    inv_freq = ROPE_THETA ** (-jnp.arange(n_axis, dtype=jnp.float32) / n_axis)
    ang = jnp.concatenate([row[:, None] * inv_freq, col[:, None] * inv_freq], axis=-1)
    return jnp.cos(ang), jnp.sin(ang)


def apply_rope(x, cos, sin):
    half = x.shape[-1] // 2
    shape = (1, x.shape[1]) + (1,) * (x.ndim - 3) + (half,)
    cos = cos.reshape(shape).astype(x.dtype)
    sin = sin.reshape(shape).astype(x.dtype)
    x1, x2 = x[..., :half], x[..., half:]
    return jnp.concatenate([x1 * cos - x2 * sin, x1 * sin + x2 * cos], axis=-1)


def map_query_blocks(fn, q):
    B, T = q.shape[:2]
    nb = T // Q_BLOCK
    qb = jnp.moveaxis(q.reshape((B, nb, Q_BLOCK) + q.shape[2:]), 1, 0)
    out = lax.map(fn, qb)
    return jnp.moveaxis(out, 0, 1).reshape((B, T) + out.shape[3:])


def gqa_attention(q, k, v):
    scale = GQA_HEAD_DIM ** -0.5

    def block(qb):
        s = jnp.einsum("bqhgd,bshd->bhgqs", qb, k).astype(jnp.float32) * scale
        p = jax.nn.softmax(s, axis=-1).astype(v.dtype)
        return jnp.einsum("bhgqs,bshd->bqhgd", p, v)

    return map_query_blocks(block, q)


def diff_attention(q, k, v, lam):
    scale = DIFF_HEAD_DIM ** -0.5

    def block(qb):
        s = jnp.einsum("bqhcd,bshcd->bhcqs", qb, k).astype(jnp.float32) * scale
        p = jax.nn.softmax(s, axis=-1)
        a = (p[:, :, 0] - lam * p[:, :, 1]).astype(v.dtype)
        return jnp.einsum("bhqs,bshe->bqhe", a, v)

    return map_query_blocks(block, q)


def short_conv_mixer(b_gate, c_gate, u, conv_w):
    g = c_gate * u
    gp = jnp.pad(g, ((0, 0), (1, 1), (0, 0)))
    y = gp[:, :-2] * conv_w[0] + gp[:, 1:-1] * conv_w[1] + gp[:, 2:] * conv_w[2]
    return b_gate * y


def retention_chunkwise(q, k, v, log_gamma, s0):
    B, T, H, _ = q.shape
    dv = v.shape[-1]
    n_chunks = T // RET_CHUNK
    pos = jnp.arange(RET_CHUNK, dtype=jnp.float32)
    lg = log_gamma.astype(jnp.float32)
    rel = pos[:, None] - pos[None, :]
    decay_mask = jnp.where(rel >= 0, jnp.exp(lg[:, None, None] * jnp.maximum(rel, 0.0)), 0.0)
    q_decay = jnp.exp(lg[None, :] * (pos[:, None] + 1.0))
    k_decay = jnp.exp(lg[None, :] * (RET_CHUNK - 1.0 - pos[:, None]))
    chunk_decay = jnp.exp(lg * RET_CHUNK)

    def to_chunks(a):
        a = a.astype(jnp.float32)
        return jnp.moveaxis(a.reshape(B, n_chunks, RET_CHUNK, H, a.shape[-1]), 1, 0)

    def step(S, blk):
        qc, kc, vc = blk
        scores = jnp.einsum("bihd,bjhd->bhij", qc, kc) * decay_mask
        o = jnp.einsum("bhij,bjhe->bihe", scores, vc)
        o = o + jnp.einsum("bihd,bhde->bihe", qc, S) * q_decay[None, :, :, None]
        S = S * chunk_decay[None, :, None, None] + jnp.einsum(
            "bjhd,bjhe->bhde", kc * k_decay[None, :, :, None], vc)
        return S, o

    s_final, o = lax.scan(step, s0.astype(jnp.float32), (to_chunks(q), to_chunks(k), to_chunks(v)))
    o = jnp.moveaxis(o, 0, 1).reshape(B, T, H, dv)
    return o, s_final


def mixer_sublayer(h, p, lambda_init, rope_gqa, rope_diff, cached):
    latent = cached is not None
    B, T, _ = h.shape
    z = h @ p["w_in"]
    (a_bg, a_cg, a_in, a_gate, g_q, g_k, g_v, g_gate,
     r_q, r_k, r_v, r_gate, d_q, d_k, d_v, d_gate) = jnp.split(z, split_points(), axis=-1)

    y_a = short_conv_mixer(a_bg, a_cg, a_in, p["conv_w"])

    gq = rms_norm(g_q.reshape(B, T, GQA_KV_HEADS, GQA_GROUP, GQA_HEAD_DIM), p["gqa_q_gain"])
    gk = rms_norm(g_k.reshape(B, T, GQA_KV_HEADS, GQA_HEAD_DIM), p["gqa_k_gain"])
    gv = g_v.reshape(B, T, GQA_KV_HEADS, GQA_HEAD_DIM)
    if latent:
        gq = apply_rope(gq, *rope_gqa)
        gk_all = jnp.concatenate([cached[0].astype(gk.dtype), apply_rope(gk, *rope_gqa)], axis=1)
        gv_all = jnp.concatenate([cached[1].astype(gv.dtype), gv], axis=1)
    else:
        gk_all, gv_all = gk, gv
    y_b = gqa_attention(gq, gk_all, gv_all).reshape(B, T, GQA_Q_W)

    rq = r_q.reshape(B, T, RET_HEADS, RET_KEY_DIM)
    rk = r_k.reshape(B, T, RET_HEADS, RET_KEY_DIM) * (RET_KEY_DIM ** -0.5)
    rv = r_v.reshape(B, T, RET_HEADS, RET_VAL_DIM)
    log_g = jax.nn.log_sigmoid(p["ret_decay"].astype(jnp.float32))
    if latent:
        s_fwd, s_bwd = cached[4][:, 0], cached[4][:, 1]
    else:
        s_fwd = s_bwd = jnp.zeros((B, RET_HEADS, RET_KEY_DIM, RET_VAL_DIM), jnp.float32)
    o_f, sf_new = retention_chunkwise(rq, rk, rv, log_g[0], s_fwd)
    o_b, sb_new = retention_chunkwise(jnp.flip(rq, 1), jnp.flip(rk, 1), jnp.flip(rv, 1), log_g[1], s_bwd)
    o_r = (o_f + jnp.flip(o_b, 1)).astype(h.dtype)
    y_c = rms_norm(o_r).reshape(B, T, RET_V_W)

    dq = d_q.reshape(B, T, DIFF_HEADS, 2, DIFF_HEAD_DIM)
    dk = d_k.reshape(B, T, DIFF_HEADS, 2, DIFF_HEAD_DIM)
    dv = d_v.reshape(B, T, DIFF_HEADS, 2 * DIFF_HEAD_DIM)
    lam_p = p["diff_lambda"].astype(jnp.float32)
    lam = jnp.exp(jnp.sum(lam_p[0] * lam_p[1])) - jnp.exp(jnp.sum(lam_p[2] * lam_p[3])) + lambda_init
    if latent:
        dq = apply_rope(dq, *rope_diff)
        dk_all = jnp.concatenate([cached[2].astype(dk.dtype), apply_rope(dk, *rope_diff)], axis=1)
        dv_all = jnp.concatenate([cached[3].astype(dv.dtype), dv], axis=1)
    else:
        dk_all, dv_all = dk, dv
    o_d = diff_attention(dq, dk_all, dv_all, lam)
    y_d = (rms_norm(o_d, p["diff_norm_gain"]) * (1.0 - lambda_init)).reshape(B, T, DIFF_V_W)

    branches = (y_a * jax.nn.silu(a_gate), y_b * jax.nn.silu(g_gate),
                y_c * jax.nn.silu(r_gate), y_d * jax.nn.silu(d_gate))
    gates = jax.nn.sigmoid(h @ p["w_mgate"])
    merged = None
    for i, y in enumerate(branches):
        term = gates[..., i * D_MODEL:(i + 1) * D_MODEL] * (y @ p["w_branch"][i])
        merged = term if merged is None else merged + term
    out = merged @ p["w_out"]
    if latent:
        return out, None
    ret_state = jnp.stack([sf_new, sb_new], axis=1).astype(h.dtype)
    return out, (gk, gv, dk, dv, ret_state)


def adaln_params(cond, w_ada_l, b_ada_l):
    mod = jax.nn.silu(cond) @ w_ada_l + b_ada_l
    return jnp.split(mod, 3, axis=-1)


def setup_inputs(seed: int = 0) -> dict:
    key = jax.random.key(seed)
    ks = jax.random.split(key, 24)
    f32 = jnp.float32

    def nrm(k, shape, s):
        return s * jax.random.normal(k, shape, f32)

    ret_logit = jnp.log(2.0 ** (5.0 + jnp.arange(RET_HEADS, dtype=f32)) - 1.0)
    return {
        "x_prompt": nrm(ks[0], (BATCH, SEQ, D_MODEL), 1.0),
        "x_sample": nrm(ks[1], (DEC_BATCH, DEC_SEQ, D_MODEL), 1.0),
        "cache_gqa_k": nrm(ks[2], (DEC_BATCH, DEPTH, PAST_LEN, GQA_KV_HEADS, GQA_HEAD_DIM), 1.0),
        "cache_gqa_v": nrm(ks[3], (DEC_BATCH, DEPTH, PAST_LEN, GQA_KV_HEADS, GQA_HEAD_DIM), 1.0),
        "cache_diff_k": nrm(ks[4], (DEC_BATCH, DEPTH, PAST_LEN, DIFF_HEADS, 2, DIFF_HEAD_DIM), 1.0),
        "cache_diff_v": nrm(ks[5], (DEC_BATCH, DEPTH, PAST_LEN, DIFF_HEADS, 2 * DIFF_HEAD_DIM), 1.0),
        "state_ret": nrm(ks[6], (DEC_BATCH, DEPTH, 2, RET_HEADS, RET_KEY_DIM, RET_VAL_DIM), 0.5),
        "c": nrm(ks[7], (DEC_BATCH, D_MODEL), 1.0),
        "c_ctx": nrm(ks[8], (D_MODEL,), 1.0),
        "w_ada": nrm(ks[9], (DEPTH, D_MODEL, 3 * D_MODEL), 0.3 * D_MODEL ** -0.5),
        "b_ada": nrm(ks[10], (DEPTH, 3 * D_MODEL), 0.01),
        "norm_gain": 1.0 + nrm(ks[11], (DEPTH, D_MODEL), 0.05),
        "w_in": nrm(ks[12], (DEPTH, D_MODEL, IN_W), D_MODEL ** -0.5),
        "conv_w": nrm(ks[13], (DEPTH, CONV_W, BRANCH_W), CONV_W ** -0.5),
        "gqa_q_gain": 1.0 + nrm(ks[14], (DEPTH, GQA_HEAD_DIM), 0.05),
        "gqa_k_gain": 1.0 + nrm(ks[15], (DEPTH, GQA_HEAD_DIM), 0.05),
        "ret_decay": ret_logit + nrm(ks[16], (DEPTH, 2, RET_HEADS), 0.1),
        "diff_lambda": nrm(ks[17], (DEPTH, 4, DIFF_HEAD_DIM), 0.1),
        "diff_norm_gain": 1.0 + nrm(ks[18], (DEPTH, 2 * DIFF_HEAD_DIM), 0.05),
        "w_branch": nrm(ks[19], (DEPTH, N_BRANCH, BRANCH_W, D_MODEL), BRANCH_W ** -0.5),
        "w_mgate": nrm(ks[20], (DEPTH, D_MODEL, N_BRANCH * D_MODEL), D_MODEL ** -0.5),
        "w_out": nrm(ks[21], (DEPTH, D_MODEL, D_MODEL), D_MODEL ** -0.5),
        "final_gain": 1.0 + nrm(ks[22], (D_MODEL,), 0.05),
    }


def reference(x_prompt, x_sample, cache_gqa_k, cache_gqa_v, cache_diff_k, cache_diff_v, state_ret,
              c, c_ctx, w_ada, b_ada, norm_gain, w_in, conv_w, gqa_q_gain, gqa_k_gain, ret_decay,
              diff_lambda, diff_norm_gain, w_branch, w_mgate, w_out, final_gain):
    n_lat = x_sample.shape[1]
    rope_gqa = axial_rope_tables(n_lat, GQA_HEAD_DIM)
    rope_diff = axial_rope_tables(n_lat, DIFF_HEAD_DIM)
    xp, xs = x_prompt, x_sample
    ctx_lists = ([], [], [], [], [])
    for l in range(DEPTH):
        p = dict(w_in=w_in[l], conv_w=conv_w[l], gqa_q_gain=gqa_q_gain[l], gqa_k_gain=gqa_k_gain[l],
                 ret_decay=ret_decay[l], diff_lambda=diff_lambda[l], diff_norm_gain=diff_norm_gain[l],
                 w_branch=w_branch[l], w_mgate=w_mgate[l], w_out=w_out[l])
        lambda_init = 0.8 - 0.6 * math.exp(-0.3 * l)
        shift, scale, gate = adaln_params(c_ctx, w_ada[l], b_ada[l])
        h = rms_norm(xp, norm_gain[l]) * (1.0 + scale) + shift
        out, ctx_tensors = mixer_sublayer(h, p, lambda_init, None, None, None)
        xp = xp + gate * out
        for lst, t in zip(ctx_lists, ctx_tensors):
            lst.append(t)
        shift, scale, gate = adaln_params(c, w_ada[l], b_ada[l])
        h = rms_norm(xs, norm_gain[l]) * (1.0 + scale[:, None]) + shift[:, None]
        cached = (cache_gqa_k[:, l], cache_gqa_v[:, l], cache_diff_k[:, l], cache_diff_v[:, l], state_ret[:, l])
        out, _ = mixer_sublayer(h, p, lambda_init, rope_gqa, rope_diff, cached)
        xs = xs + gate[:, None] * out
    y_prompt = rms_norm(xp, final_gain)
    y_sample = rms_norm(xs, final_gain)
    new_cache_gqa_k = jnp.stack(ctx_lists[0], axis=1)
    new_cache_gqa_v = jnp.stack(ctx_lists[1], axis=1)
    new_cache_diff_k = jnp.stack(ctx_lists[2], axis=1)
    new_cache_diff_v = jnp.stack(ctx_lists[3], axis=1)
    new_state_ret = jnp.stack(ctx_lists[4], axis=1)
    return (y_prompt, y_sample, new_cache_gqa_k, new_cache_gqa_v, new_cache_diff_k, new_cache_diff_v, new_state_ret)
```

```python
import functools
import math

import jax
import jax.numpy as jnp
from jax import lax
from jax.experimental import pallas as pl
from jax.experimental.pallas import tpu as pltpu

F32 = jnp.float32
BF16 = jnp.bfloat16

GRID_W = 64
BRANCH_W = 256
HEAD_W = 64
GQA_KV_HEADS = 2
RET_HEADS = 4
DIFF_HEADS = 4
DIFF_HEAD_DIM = 32
ROPE_THETA = 10000.0
EPS = 1e-6
N_BRANCH = 4
LANES = 128
VMEM_LIMIT_BYTES = 56 * 1024 * 1024

COL_A = 0
COL_B = 1024
COL_C = 1792
COL_D = 2816
W_IN_PACKED = 4096


def _silu(x):
    return x * jax.nn.sigmoid(x)


def _dot(a, b):
    return jnp.dot(a, b, preferred_element_type=F32)


def _group_ones(width):
    r = lax.broadcasted_iota(jnp.int32, (width, width), 0) >> 6
    c = lax.broadcasted_iota(jnp.int32, (width, width), 1) >> 6
    return jnp.where(r == c, 1.0, 0.0).astype(BF16)


def _head_rms(z, ones):
    zz = z * z
    hi = zz.astype(BF16)
    lo = (zz - hi.astype(F32)).astype(BF16)
    ss = _dot(hi, ones) + _dot(lo, ones)
    return z * lax.rsqrt(ss * (1.0 / HEAD_W) + EPS)


def _rope(x, cos_t, sin_t, half):
    lane = lax.broadcasted_iota(jnp.int32, (x.shape[0], LANES), 1)
    first = (lane & (2 * half - 1)) < half
    outs = []
    for j in range(x.shape[1] // LANES):
        xc = x[:, j * LANES:(j + 1) * LANES]
        partner = jnp.where(first, pltpu.roll(xc, LANES - half, 1), pltpu.roll(xc, half, 1))
        outs.append(xc * cos_t + partner * sin_t)
    return outs[0] if len(outs) == 1 else jnp.concatenate(outs, axis=1)


def _modulated_norm(x, mod_ref, ng_ref):
    ms = jnp.mean(x * x, axis=-1, keepdims=True)
    y = x * lax.rsqrt(ms + EPS) * ng_ref[...]
    return y * (1.0 + mod_ref[0, 1:2, :]) + mod_ref[0, 0:1, :]


def _adaln_kernel(cond_ref, w_ref, b_ref, o_ref):
    a = _silu(cond_ref[...]).astype(BF16)
    o_ref[0] = _dot(a, w_ref[0]) + b_ref[0]


def _adaln(cond, w_ada, b_ada):
    depth, d, d3 = w_ada.shape
    rows = cond.shape[0]
    return pl.pallas_call(
        _adaln_kernel,
        out_shape=jax.ShapeDtypeStruct((depth, rows, d3), F32),
        grid=(depth,),
        in_specs=[pl.BlockSpec((rows, d), lambda l: (0, 0)),
                  pl.BlockSpec((1, d, d3), lambda l: (l, 0, 0)),
                  pl.BlockSpec((1, 1, d3), lambda l: (l, 0, 0))],
        out_specs=pl.BlockSpec((1, rows, d3), lambda l: (l, 0, 0)),
        compiler_params=pltpu.CompilerParams(vmem_limit_bytes=VMEM_LIMIT_BYTES),
        name="adaln",
    )(cond, w_ada, b_ada.reshape(depth, 1, d3))


def _inproj_kernel(*refs, latent):
    if latent:
        (x_ref, mod_ref, ng_ref, w_ref, qg_ref, kg_ref, c64_ref, s64_ref, c32_ref, s32_ref,
         oa_ref, og_ref, qb_ref, kb_ref, vb_ref, qc_ref, kc_ref, vc_ref, qd_ref, kd_ref, vd_ref) = refs
    else:
        (x_ref, mod_ref, ng_ref, w_ref, qg_ref, kg_ref,
         oa_ref, og_ref, qb_ref, kb_ref, vb_ref, qc_ref, kc_ref, vc_ref, qd_ref, kd_ref, vd_ref,
         ck_ref, cv_ref, cdk_ref, cdv_ref) = refs
    hb = _modulated_norm(x_ref[0], mod_ref, ng_ref).astype(BF16)

    z = _dot(hb, w_ref[:, COL_A:COL_A + 1024])
    oa_ref[0, :, 0:256] = (z[:, 0:256] * _silu(z[:, 768:1024])).astype(BF16)
    oa_ref[0, :, 256:512] = (z[:, 256:512] * z[:, 512:768]).astype(BF16)

    z = _dot(hb, w_ref[:, COL_B:COL_B + 768])
    q = _head_rms(z[:, 0:256], _group_ones(256)) * qg_ref[...]
    k = _head_rms(z[:, 256:384], _group_ones(128)) * kg_ref[...]
    v = z[:, 384:512]
    if latent:
        q = _rope(q, c64_ref[...], s64_ref[...], 32)
        k_att = _rope(k, c64_ref[...], s64_ref[...], 32)
    else:
        k_att = k
        ck_ref[0] = k
        cv_ref[0] = v
    qb_ref[0] = (q * (HEAD_W ** -0.5)).astype(BF16)
    for h in range(GQA_KV_HEADS):
        kb_ref[0, h] = k_att[:, h * HEAD_W:(h + 1) * HEAD_W].astype(BF16)
        vb_ref[0, h] = v[:, h * HEAD_W:(h + 1) * HEAD_W].astype(BF16)
    og_ref[0, :, 0:256] = _silu(z[:, 512:768]).astype(BF16)

    z = _dot(hb, w_ref[:, COL_C:COL_C + 1024])
    qc_ref[0] = z[:, 0:256].astype(BF16)
    kc_ref[0] = (z[:, 256:512] * (HEAD_W ** -0.5)).astype(BF16)
    vc_ref[0] = z[:, 512:768].astype(BF16)
    og_ref[0, :, 256:512] = _silu(z[:, 768:1024]).astype(BF16)

    z = _dot(hb, w_ref[:, COL_D:COL_D + 1280])
    q = z[:, 0:512]
    k = z[:, 512:768]
    v = z[:, 768:1024]
    if latent:
        q = _rope(q, c32_ref[...], s32_ref[...], 16)
        k_att = _rope(k, c32_ref[...], s32_ref[...], 16)
    else:
        k_att = k
        cdk_ref[0] = k
        cdv_ref[0] = v
    qd_ref[0] = (q * (DIFF_HEAD_DIM ** -0.5)).astype(BF16)
    for h in range(DIFF_HEADS):
        kd_ref[0, h] = k_att[:, h * HEAD_W:(h + 1) * HEAD_W].astype(BF16)
        vd_ref[0, h] = v[:, h * HEAD_W:(h + 1) * HEAD_W].astype(BF16)
    og_ref[0, :, 512:768] = _silu(z[:, 1024:1280]).astype(BF16)


def _inproj(x, mod, ng, w_packed, qg, kg, rope_tabs, *, latent, tm):
    b, t, d = x.shape
    nt = t // tm
    per_batch_mod = mod.shape[0] > 1
    tok = lambda w: pl.BlockSpec((1, tm, w), lambda i, j: (i, j, 0))
    heads = lambda n: pl.BlockSpec((1, n, tm, HEAD_W), lambda i, j: (i, 0, j, 0))
    const = lambda shape: pl.BlockSpec(shape, lambda i, j: (0,) * len(shape))
    in_specs = [tok(d),
                pl.BlockSpec((1, 3, d), (lambda i, j: (i, 0, 0)) if per_batch_mod else (lambda i, j: (0, 0, 0))),
                const((1, d)), const(w_packed.shape), const((1, 256)), const((1, 128))]
    args = [x, mod, ng, w_packed, qg, kg]
    if latent:
        in_specs += [pl.BlockSpec((tm, LANES), lambda i, j: (j, 0))] * 4
        args += list(rope_tabs)
    sd = jax.ShapeDtypeStruct
    out_shape = [sd((b, t, 512), BF16), sd((b, t, 768), BF16),
                 sd((b, t, 256), BF16), sd((b, GQA_KV_HEADS, t, HEAD_W), BF16), sd((b, GQA_KV_HEADS, t, HEAD_W), BF16),
                 sd((b, t, 256), BF16), sd((b, t, 256), BF16), sd((b, t, 256), BF16),
                 sd((b, t, 512), BF16), sd((b, DIFF_HEADS, t, HEAD_W), BF16), sd((b, DIFF_HEADS, t, HEAD_W), BF16)]
    out_specs = [tok(512), tok(768), tok(256), heads(GQA_KV_HEADS), heads(GQA_KV_HEADS),
                 tok(256), tok(256), tok(256), tok(512), heads(DIFF_HEADS), heads(DIFF_HEADS)]
    if not latent:
        out_shape += [sd((b, t, 128), F32), sd((b, t, 128), F32), sd((b, t, 256), F32), sd((b, t, 256), F32)]
        out_specs += [tok(128), tok(128), tok(256), tok(256)]
    return pl.pallas_call(
        functools.partial(_inproj_kernel, latent=latent),
        out_shape=out_shape,
        grid=(b, nt),
        in_specs=in_specs,
        out_specs=out_specs,
        compiler_params=pltpu.CompilerParams(
            dimension_semantics=("parallel", "parallel"), vmem_limit_bytes=VMEM_LIMIT_BYTES),
        name="inproj_lat" if latent else "inproj_ctx",
    )(*args)


def _attn_kernel(q_ref, k_ref, v_ref, lam_ref, gain_ref, o_ref, *, n_kv, tq, tk, n_chunks, diff, post_scale):
    outs = []
    for h in range(n_kv):
        q2 = q_ref[0, :, h * LANES:(h + 1) * LANES]
        qs = jnp.concatenate([q2[:, :HEAD_W], q2[:, HEAD_W:]], axis=0)

        def body(c, carry, qs=qs, h=h):
            m, l, acc = carry
            start = pl.multiple_of(c * tk, tk)
            kc = k_ref[0, h, pl.ds(start, tk), :]
            vc = v_ref[0, h, pl.ds(start, tk), :]
            s = lax.dot_general(qs, kc, (((1,), (1,)), ((), ())), preferred_element_type=F32)
            m_new = jnp.maximum(m, jnp.max(s, axis=-1, keepdims=True))
            alpha = jnp.exp(m - m_new)
            p = jnp.exp(s - m_new)
            l = alpha * l + jnp.sum(p, axis=-1, keepdims=True)
            acc = alpha * acc + _dot(p.astype(BF16), vc)
            return m_new, l, acc

        init = (jnp.full((2 * tq, 1), -jnp.inf, F32), jnp.zeros((2 * tq, 1), F32),
                jnp.zeros((2 * tq, HEAD_W), F32))
        _, l, acc = lax.fori_loop(0, n_chunks, body, init)
        o = acc / l
        if diff:
            lp = lam_ref[...]
            lam = (jnp.exp(jnp.sum(lp[0:1] * lp[1:2], axis=-1, keepdims=True))
                   - jnp.exp(jnp.sum(lp[2:3] * lp[3:4], axis=-1, keepdims=True)) + (1.0 - post_scale))
            od = o[:tq] - lam * o[tq:]
            ms = jnp.mean(od * od, axis=-1, keepdims=True)
            outs.append(od * lax.rsqrt(ms + EPS) * gain_ref[...] * post_scale)
        else:
            outs += [o[:tq], o[tq:]]
    o_ref[0] = jnp.concatenate(outs, axis=-1).astype(o_ref.dtype)


def _attention(q, k, v, lam_p, gain, *, diff, post_scale, tq, tk):
    b, t, qw = q.shape
    n_kv, s = k.shape[1], k.shape[2]
    assert qw == n_kv * LANES and s % tk == 0 and t % tq == 0
    out_w = n_kv * HEAD_W if diff else qw
    kern = functools.partial(_attn_kernel, n_kv=n_kv, tq=tq, tk=tk, n_chunks=s // tk, diff=diff,
                             post_scale=post_scale)
    return pl.pallas_call(
        kern,
        out_shape=jax.ShapeDtypeStruct((b, t, out_w), BF16),
        grid=(b, t // tq),
        in_specs=[pl.BlockSpec((1, tq, qw), lambda i, j: (i, j, 0)),
                  pl.BlockSpec((1, n_kv, s, HEAD_W), lambda i, j: (i, 0, 0, 0)),
                  pl.BlockSpec((1, n_kv, s, HEAD_W), lambda i, j: (i, 0, 0, 0)),
                  pl.BlockSpec(lam_p.shape, lambda i, j: (0, 0)),
                  pl.BlockSpec(gain.shape, lambda i, j: (0, 0))],
        out_specs=pl.BlockSpec((1, tq, out_w), lambda i, j: (i, j, 0)),
        compiler_params=pltpu.CompilerParams(
            dimension_semantics=("parallel", "parallel"), vmem_limit_bytes=VMEM_LIMIT_BYTES),
        name="diff_attn" if diff else "gqa_attn",
    )(q, k, v, lam_p, gain)


def _retention_kernel(q_ref, k_ref, v_ref, s0_ref, dec_ref, o_ref, sn_ref, sb_scr, *, chunk, n_chunks):
    c_len = chunk
    dec = dec_ref[...]
    lg = jnp.minimum(dec, 0.0) - jnp.log1p(jnp.exp(-jnp.abs(dec)))
    row = lax.broadcasted_iota(jnp.int32, (c_len, c_len), 0)
    col = lax.broadcasted_iota(jnp.int32, (c_len, c_len), 1)
    rel = (row - col).astype(F32)
    pos = lax.broadcasted_iota(jnp.int32, (c_len, HEAD_W), 0).astype(F32)
    masks, qdf, qdb, kdf, kdb, cdf, cdb = [], [], [], [], [], [], []
    for h in range(RET_HEADS):
        lf = lg[0:1, h:h + 1]
        lb = lg[1:2, h:h + 1]
        masks.append(jnp.where(rel >= 0, jnp.exp(lf * jnp.maximum(rel, 0.0)), 0.0)
                     + jnp.where(rel <= 0, jnp.exp(lb * jnp.maximum(-rel, 0.0)), 0.0))
        qdf.append(jnp.exp(lf * (pos + 1.0)))
        qdb.append(jnp.exp(lb * (c_len - pos)))
        kdf.append(jnp.exp(lf * (c_len - 1.0 - pos)))
        kdb.append(jnp.exp(lb * pos))
        cdf.append(jnp.exp(lf * c_len))
        cdb.append(jnp.exp(lb * c_len))

    def chunk_of(ref, c, h):
        start = pl.multiple_of(c * c_len, c_len)
        return ref[0, pl.ds(start, c_len), h * HEAD_W:(h + 1) * HEAD_W]

    def kv_state(kc, vc, kdecay):
        kd = (kc.astype(F32) * kdecay).astype(BF16)
        return lax.dot_general(kd, vc, (((0,), (0,)), ((), ())), preferred_element_type=F32)

    def bwd_body(i, states):
        c = n_chunks - 1 - i
        new = []
        for h in range(RET_HEADS):
            sb_scr[c, h] = states[h]
            new.append(states[h] * cdb[h] + kv_state(chunk_of(k_ref, c, h), chunk_of(v_ref, c, h), kdb[h]))
        return tuple(new)

    sb_fin = lax.fori_loop(0, n_chunks, bwd_body, tuple(s0_ref[0, 1, h] for h in range(RET_HEADS)))

    def fwd_body(c, states):
        new, outs = [], []
        for h in range(RET_HEADS):
            qc, kc, vc = chunk_of(q_ref, c, h), chunk_of(k_ref, c, h), chunk_of(v_ref, c, h)
            a = lax.dot_general(qc, kc, (((1,), (1,)), ((), ())), preferred_element_type=F32) * masks[h]
            o = _dot(a.astype(BF16), vc)
            o = o + _dot(qc, states[h].astype(BF16)) * qdf[h]
            o = o + _dot(qc, sb_scr[c, h].astype(BF16)) * qdb[h]
            ms = jnp.mean(o * o, axis=-1, keepdims=True)
            outs.append(o * lax.rsqrt(ms + EPS))
            new.append(states[h] * cdf[h] + kv_state(kc, vc, kdf[h]))
        start = pl.multiple_of(c * c_len, c_len)
        o_ref[0, pl.ds(start, c_len), :] = jnp.concatenate(outs, axis=-1).astype(o_ref.dtype)
        return tuple(new)

    sf_fin = lax.fori_loop(0, n_chunks, fwd_body, tuple(s0_ref[0, 0, h] for h in range(RET_HEADS)))
    for h in range(RET_HEADS):
        sn_ref[0, 0, h] = sf_fin[h]
        sn_ref[0, 1, h] = sb_fin[h]


def _retention(q, k, v, s0, decay, *, chunk):
    b, t, w = q.shape
    n_chunks = t // chunk
    tok = pl.BlockSpec((1, t, w), lambda i: (i, 0, 0))
    st = pl.BlockSpec((1, 2, RET_HEADS, HEAD_W, HEAD_W), lambda i: (i, 0, 0, 0, 0))
    return pl.pallas_call(
        functools.partial(_retention_kernel, chunk=chunk, n_chunks=n_chunks),
        out_shape=[jax.ShapeDtypeStruct((b, t, w), BF16),
                   jax.ShapeDtypeStruct((b, 2, RET_HEADS, HEAD_W, HEAD_W), F32)],
        grid=(b,),
        in_specs=[tok, tok, tok, st, pl.BlockSpec((2, RET_HEADS), lambda i: (0, 0))],
        out_specs=[tok, st],
        scratch_shapes=[pltpu.VMEM((n_chunks, RET_HEADS, HEAD_W, HEAD_W), F32)],
        compiler_params=pltpu.CompilerParams(
            dimension_semantics=("parallel",), vmem_limit_bytes=VMEM_LIMIT_BYTES),
        name="retention",
    )(q, k, v, s0, decay)


def _merge_kernel(x_ref, mod_ref, ng_ref, oa_ref, prev_ref, next_ref, og_ref, yb_ref, yc_ref, yd_ref,
                  cw_ref, wg_ref, wbr_ref, wo_ref, fg_ref, o_ref, *, tm, last):
    j = pl.program_id(1)
    x = x_ref[0]
    hb = _modulated_norm(x, mod_ref, ng_ref).astype(BF16)

    g = oa_ref[0, :, 256:512].astype(F32)
    row = lax.broadcasted_iota(jnp.int32, g.shape, 0)
    g_first = jnp.where(j == 0, 0.0, prev_ref[0, 7:8, 256:512].astype(F32))
    g_last = jnp.where(j == pl.num_programs(1) - 1, 0.0, next_ref[0, 0:1, 256:512].astype(F32))
    g_prev = jnp.where(row == 0, g_first, pltpu.roll(g, 1, 0))
    g_next = jnp.where(row == tm - 1, g_last, pltpu.roll(g, tm - 1, 0))
    conv = g_prev * cw_ref[0:1, :] + g * cw_ref[1:2, :] + g_next * cw_ref[2:3, :]
    ys = [(oa_ref[0, :, 0:256].astype(F32) * conv).astype(BF16),
          yb_ref[0] * og_ref[0, :, 0:256],
          yc_ref[0] * og_ref[0, :, 256:512],
          yd_ref[0] * og_ref[0, :, 512:768]]

    merged = None
    for i in range(N_BRANCH):
        gate = jax.nn.sigmoid(_dot(hb, wg_ref[:, i * 1024:(i + 1) * 1024]))
        term = gate * _dot(ys[i], wbr_ref[i])
        merged = term if merged is None else merged + term
    out = _dot(merged.astype(BF16), wo_ref[...])
    xn = x + mod_ref[0, 2:3, :] * out
    if last:
        ms = jnp.mean(xn * xn, axis=-1, keepdims=True)
        xn = xn * lax.rsqrt(ms + EPS) * fg_ref[...]
    o_ref[0] = xn


def _merge(x, mod, ng, oa, og, yb, yc, yd, conv_w, wg, wbr, wo, fg, *, tm, last):
    b, t, d = x.shape
    nt = t // tm
    rb = tm // 8
    per_batch_mod = mod.shape[0] > 1
    tok = lambda w: pl.BlockSpec((1, tm, w), lambda i, j: (i, j, 0))
    const = lambda shape: pl.BlockSpec(shape, lambda i, j: (0,) * len(shape))
    in_specs = [tok(d),
                pl.BlockSpec((1, 3, d), (lambda i, j: (i, 0, 0)) if per_batch_mod else (lambda i, j: (0, 0, 0))),
                const((1, d)), tok(512),
                pl.BlockSpec((1, 8, 512), lambda i, j: (i, jnp.maximum(j * rb - 1, 0), 0)),
                pl.BlockSpec((1, 8, 512), lambda i, j: (i, jnp.minimum((j + 1) * rb, t // 8 - 1), 0)),
                tok(768), tok(256), tok(256), tok(256),
                const(conv_w.shape), const(wg.shape), const(wbr.shape), const(wo.shape), const((1, d))]
    return pl.pallas_call(
        functools.partial(_merge_kernel, tm=tm, last=last),
        out_shape=jax.ShapeDtypeStruct((b, t, d), F32),
        grid=(b, nt),
        in_specs=in_specs,
        out_specs=tok(d),
        compiler_params=pltpu.CompilerParams(
            dimension_semantics=("parallel", "parallel"), vmem_limit_bytes=VMEM_LIMIT_BYTES),
        name="merge",
    )(x, mod, ng, oa, oa, oa, og, yb, yc, yd, conv_w, wg, wbr, wo, fg)


def _pack_w_in(w):
    d = w.shape[0]
    dq = w[:, 2816:3072].reshape(d, DIFF_HEADS, 2, 1, DIFF_HEAD_DIM)
    eye = jnp.eye(2, dtype=w.dtype).reshape(1, 1, 2, 2, 1)
    dq = (dq * eye).reshape(d, 2 * 256)
    return jnp.concatenate([w[:, :2816], dq, w[:, 3072:]], axis=1).astype(BF16)


def _rope_tables(n_tokens, head_dim):
    rows = n_tokens // GRID_W
    row = jnp.repeat(jnp.arange(rows, dtype=F32), GRID_W)
    col = jnp.tile(jnp.arange(GRID_W, dtype=F32), rows)
    n_axis = head_dim // 4
    inv_freq = ROPE_THETA ** (-jnp.arange(n_axis, dtype=F32) / n_axis)
    ang = jnp.concatenate([row[:, None] * inv_freq, col[:, None] * inv_freq], axis=-1)
    cos, sin = jnp.cos(ang), jnp.sin(ang)
    reps = LANES // head_dim
    return (jnp.tile(jnp.concatenate([cos, cos], axis=-1), (1, reps)),
            jnp.tile(jnp.concatenate([-sin, sin], axis=-1), (1, reps)))


def _pick(n, pref):
    return pref if n % pref == 0 else n


def kernel(x_prompt, x_sample, cache_gqa_k, cache_gqa_v, cache_diff_k, cache_diff_v, state_ret, c, c_ctx,
           w_ada, b_ada, norm_gain, w_in, conv_w, gqa_q_gain, gqa_k_gain, ret_decay, diff_lambda,
           diff_norm_gain, w_branch, w_mgate, w_out, final_gain):
    depth = w_in.shape[0]
    bc, tc, d = x_prompt.shape
    bl, tl, _ = x_sample.shape

    cond = jnp.concatenate([c_ctx[None], c], axis=0)
    n_cond = cond.shape[0]
    cond = jnp.pad(cond, ((0, -n_cond % 8), (0, 0)))
    mod = _adaln(cond, w_ada.astype(BF16), b_ada)
    mod = mod.reshape(depth, cond.shape[0], 3, d)

    rope = _rope_tables(tl, HEAD_W) + _rope_tables(tl, DIFF_HEAD_DIM)
    fg = final_gain.reshape(1, d)
    zero_state = jnp.zeros((bc, 2, RET_HEADS, HEAD_W, HEAD_W), F32)

    def to_heads(cache):
        b, p = cache.shape[:2]
        return jnp.transpose(cache.reshape(b, p, -1, HEAD_W), (0, 2, 1, 3)).astype(BF16)

    xp, xs = x_prompt, x_sample
    new_caches = ([], [], [], [], [])
    for l in range(depth):
        lambda_init = 0.8 - 0.6 * math.exp(-0.3 * l)
        w_packed = _pack_w_in(w_in[l])
        ng = norm_gain[l].reshape(1, d)
        qg = jnp.tile(gqa_q_gain[l], 256 // HEAD_W).reshape(1, 256)
        kg = jnp.tile(gqa_k_gain[l], 128 // HEAD_W).reshape(1, 128)
        dgain = diff_norm_gain[l].reshape(1, HEAD_W)
        wg, wbr, wo = w_mgate[l].astype(BF16), w_branch[l].astype(BF16), w_out[l].astype(BF16)

        for latent in (False, True):
            x = xs if latent else xp
            b, t, _ = x.shape
            m = mod[l, 1:1 + bl] if latent else mod[l, 0:1]
            tm = _pick(t, 512) if latent else _pick(t, 256)
            outs = _inproj(x, m, ng, w_packed, qg, kg, rope, latent=latent, tm=tm)
            oa, og, qb, kb, vb, qc, kc, vc, qd, kd, vd = outs[:11]
            if latent:
                kb = jnp.concatenate([to_heads(cache_gqa_k[:, l]), kb], axis=2)
                vb = jnp.concatenate([to_heads(cache_gqa_v[:, l]), vb], axis=2)
                kd = jnp.concatenate([to_heads(cache_diff_k[:, l]), kd], axis=2)
                vd = jnp.concatenate([to_heads(cache_diff_v[:, l]), vd], axis=2)
                s0 = state_ret[:, l]
            else:
                for lst, a in zip(new_caches[:4], outs[11:]):
                    lst.append(a)
                s0 = zero_state
            s_len = kb.shape[2]
            tq, tk = _pick(t, 256), _pick(s_len, 512)
            yb = _attention(qb, kb, vb, diff_lambda[l], dgain, diff=False, post_scale=1.0, tq=tq, tk=tk)
            yd = _attention(qd, kd, vd, diff_lambda[l], dgain, diff=True, post_scale=1.0 - lambda_init,
                            tq=tq, tk=tk)
            yc, s_new = _retention(qc, kc, vc, s0, ret_decay[l], chunk=_pick(t, 256))
            if not latent:
                new_caches[4].append(s_new)
            x = _merge(x, m, ng, oa, og, yb, yc, yd, conv_w[l], wg, wbr, wo, fg, tm=tm, last=(l == depth - 1))
            if latent:
                xs = x
            else:
                xp = x

    ck = jnp.stack(new_caches[0], axis=1).reshape(bc, depth, tc, GQA_KV_HEADS, HEAD_W)
    cv = jnp.stack(new_caches[1], axis=1).reshape(bc, depth, tc, GQA_KV_HEADS, HEAD_W)
    cdk = jnp.stack(new_caches[2], axis=1).reshape(bc, depth, tc, DIFF_HEADS, 2, DIFF_HEAD_DIM)
    cdv = jnp.stack(new_caches[3], axis=1).reshape(bc, depth, tc, DIFF_HEADS, 2 * DIFF_HEAD_DIM)
    cs = jnp.stack(new_caches[4], axis=1)
    return (xp, xs, ck, cv, cdk, cdv, cs)
```

```python
import functools
import math

import jax
import jax.numpy as jnp
from jax import lax
from jax.experimental import pallas as pl
from jax.experimental.pallas import tpu as pltpu

F32 = jnp.float32
BF16 = jnp.bfloat16

GRID_W = 64
BRANCH_W = 256
HEAD_W = 64
GQA_KV_HEADS = 2
RET_HEADS = 4
DIFF_HEADS = 4
DIFF_HEAD_DIM = 32
ROPE_THETA = 10000.0
EPS = 1e-6
LOG2_E = math.log2(math.e)
N_BRANCH = 4
LANES = 128
Q_TILE = 256
KEY_BLOCK = 512
VMEM_LIMIT_BYTES = 56 * 1024 * 1024

COL_A = 0
COL_B = 1024
COL_C = 1792
COL_D = 2816
W_IN_PACKED = 4096


def _silu(x):
    return x * jax.nn.sigmoid(x)


def _dot(a, b):
    return jnp.dot(a, b, preferred_element_type=F32)


def _group_ones(width):
    r = lax.broadcasted_iota(jnp.int32, (width, width), 0) >> 6
    c = lax.broadcasted_iota(jnp.int32, (width, width), 1) >> 6
    return jnp.where(r == c, 1.0, 0.0).astype(BF16)


def _head_rms(z, ones):
    zz = z * z
    hi = zz.astype(BF16)
    lo = (zz - hi.astype(F32)).astype(BF16)
    ss = _dot(hi, ones) + _dot(lo, ones)
    return z * lax.rsqrt(ss * (1.0 / HEAD_W) + EPS)


def _rope(x, cos_t, sin_t, half):
    lane = lax.broadcasted_iota(jnp.int32, (x.shape[0], LANES), 1)
    first = (lane & (2 * half - 1)) < half
    outs = []
    for j in range(x.shape[1] // LANES):
        xc = x[:, j * LANES:(j + 1) * LANES]
        partner = jnp.where(first, pltpu.roll(xc, LANES - half, 1), pltpu.roll(xc, half, 1))
        outs.append(xc * cos_t + partner * sin_t)
    return outs[0] if len(outs) == 1 else jnp.concatenate(outs, axis=1)


def _modulated_norm(x, mod_ref, ng_ref):
    ms = jnp.mean(x * x, axis=-1, keepdims=True)
    y = x * lax.rsqrt(ms + EPS) * ng_ref[...]
    return y * (1.0 + mod_ref[0, 1:2, :]) + mod_ref[0, 0:1, :]


def _adaln_kernel(cond_ref, w_ref, b_ref, o_ref):
    a = _silu(cond_ref[...]).astype(BF16)
    o_ref[0] = _dot(a, w_ref[0]) + b_ref[0]


def _adaln(cond, w_ada, b_ada):
    depth, d, d3 = w_ada.shape
    rows = cond.shape[0]
    return pl.pallas_call(
        _adaln_kernel,
        out_shape=jax.ShapeDtypeStruct((depth, rows, d3), F32),
        grid=(depth,),
        in_specs=[pl.BlockSpec((rows, d), lambda l: (0, 0)),
                  pl.BlockSpec((1, d, d3), lambda l: (l, 0, 0)),
                  pl.BlockSpec((1, 1, d3), lambda l: (l, 0, 0))],
        out_specs=pl.BlockSpec((1, rows, d3), lambda l: (l, 0, 0)),
        compiler_params=pltpu.CompilerParams(vmem_limit_bytes=VMEM_LIMIT_BYTES),
        name="adaln",
    )(cond, w_ada, b_ada.reshape(depth, 1, d3))


def _store_q_tiles(ref, q):
    qt = q.T.astype(BF16)
    for c in range(qt.shape[1] // Q_TILE):
        ref[0, c] = qt[:, c * Q_TILE:(c + 1) * Q_TILE]


def _inproj_kernel(*refs, latent):
    if latent:
        (x_ref, mod_ref, ng_ref, w_ref, qg_ref, kg_ref, c64_ref, s64_ref, c32_ref, s32_ref,
         oa_ref, og_ref, qb_ref, kb_ref, vb_ref, qc_ref, kc_ref, vc_ref, qd_ref, kd_ref, vd_ref) = refs
    else:
        (x_ref, mod_ref, ng_ref, w_ref, qg_ref, kg_ref,
         oa_ref, og_ref, qb_ref, kb_ref, vb_ref, qc_ref, kc_ref, vc_ref, qd_ref, kd_ref, vd_ref,
         ck_ref, cv_ref, cdk_ref, cdv_ref) = refs
    hb = _modulated_norm(x_ref[0], mod_ref, ng_ref).astype(BF16)

    z = _dot(hb, w_ref[:, COL_A:COL_A + 1024])
    oa_ref[0, :, 0:256] = (z[:, 0:256] * _silu(z[:, 768:1024])).astype(BF16)
    oa_ref[0, :, 256:512] = (z[:, 256:512] * z[:, 512:768]).astype(BF16)

    z = _dot(hb, w_ref[:, COL_B:COL_B + 768])
    q = _head_rms(z[:, 0:256], _group_ones(256)) * qg_ref[...]
    k = _head_rms(z[:, 256:384], _group_ones(128)) * kg_ref[...]
    v = z[:, 384:512]
    if latent:
        q = _rope(q, c64_ref[...], s64_ref[...], 32)
        k_att = _rope(k, c64_ref[...], s64_ref[...], 32)
    else:
        k_att = k
        ck_ref[0] = k
        cv_ref[0] = v
    _store_q_tiles(qb_ref, q * (HEAD_W ** -0.5 * LOG2_E))
    vb_ref[0] = v.T.astype(BF16)
    for h in range(GQA_KV_HEADS):
        kb_ref[0, h] = k_att[:, h * HEAD_W:(h + 1) * HEAD_W].astype(BF16)
    og_ref[0, :, 0:256] = _silu(z[:, 512:768]).astype(BF16)

    z = _dot(hb, w_ref[:, COL_C:COL_C + 1024])
    qc_ref[0] = z[:, 0:256].astype(BF16)
    kc_ref[0] = (z[:, 256:512] * (HEAD_W ** -0.5)).astype(BF16)
    vc_ref[0] = z[:, 512:768].astype(BF16)
    og_ref[0, :, 256:512] = _silu(z[:, 768:1024]).astype(BF16)

    z = _dot(hb, w_ref[:, COL_D:COL_D + 1280])
    q = z[:, 0:512]
    k = z[:, 512:768]
    v = z[:, 768:1024]
    if latent:
        q = _rope(q, c32_ref[...], s32_ref[...], 16)
        k_att = _rope(k, c32_ref[...], s32_ref[...], 16)
    else:
        k_att = k
        cdk_ref[0] = k
        cdv_ref[0] = v
    _store_q_tiles(qd_ref, q * (DIFF_HEAD_DIM ** -0.5 * LOG2_E))
    vd_ref[0] = v.T.astype(BF16)
    for h in range(DIFF_HEADS):
        kd_ref[0, h] = k_att[:, h * HEAD_W:(h + 1) * HEAD_W].astype(BF16)
    og_ref[0, :, 512:768] = _silu(z[:, 1024:1280]).astype(BF16)


def _inproj(x, mod, ng, w_packed, qg, kg, rope_tabs, *, latent, tm):
    b, t, d = x.shape
    nt = t // tm
    per_batch_mod = mod.shape[0] > 1
    tok = lambda w: pl.BlockSpec((1, tm, w), lambda i, j: (i, j, 0))
    tok_t = lambda w: pl.BlockSpec((1, w, tm), lambda i, j: (i, 0, j))
    heads = lambda n: pl.BlockSpec((1, n, tm, HEAD_W), lambda i, j: (i, 0, j, 0))
    const = lambda shape: pl.BlockSpec(shape, lambda i, j: (0,) * len(shape))
    in_specs = [tok(d),
                pl.BlockSpec((1, 3, d), (lambda i, j: (i, 0, 0)) if per_batch_mod else (lambda i, j: (0, 0, 0))),
                const((1, d)), const(w_packed.shape), const((1, 256)), const((1, 128))]
    args = [x, mod, ng, w_packed, qg, kg]
    if latent:
        in_specs += [pl.BlockSpec((tm, LANES), lambda i, j: (j, 0))] * 4
        args += list(rope_tabs)
    sd = jax.ShapeDtypeStruct
    qn = tm // Q_TILE
    q_tiles = lambda w: pl.BlockSpec((1, qn, w, Q_TILE), lambda i, j: (i, j, 0, 0))
    out_shape = [sd((b, t, 512), BF16), sd((b, t, 768), BF16),
                 sd((b, t // Q_TILE, 256, Q_TILE), BF16), sd((b, GQA_KV_HEADS, t, HEAD_W), BF16), sd((b, 128, t), BF16),
                 sd((b, t, 256), BF16), sd((b, t, 256), BF16), sd((b, t, 256), BF16),
                 sd((b, t // Q_TILE, 512, Q_TILE), BF16), sd((b, DIFF_HEADS, t, HEAD_W), BF16), sd((b, 256, t), BF16)]
    out_specs = [tok(512), tok(768), q_tiles(256), heads(GQA_KV_HEADS), tok_t(128),
                 tok(256), tok(256), tok(256), q_tiles(512), heads(DIFF_HEADS), tok_t(256)]
    if not latent:
        out_shape += [sd((b, t, 128), F32), sd((b, t, 128), F32), sd((b, t, 256), F32), sd((b, t, 256), F32)]
        out_specs += [tok(128), tok(128), tok(256), tok(256)]
    return pl.pallas_call(
        functools.partial(_inproj_kernel, latent=latent),
        out_shape=out_shape,
        grid=(b, nt),
        in_specs=in_specs,
        out_specs=out_specs,
        compiler_params=pltpu.CompilerParams(
            dimension_semantics=("parallel", "parallel"), vmem_limit_bytes=VMEM_LIMIT_BYTES),
        name="inproj_lat" if latent else "inproj_ctx",
    )(*args)


def _attn_kernel(qt_ref, k_ref, vt_ref, lam_ref, gain_ref, o_ref, s_even, s_odd, *, n_kv, n_q, kb, diff,
                 lambda_init):
    s_len = k_ref.shape[2]
    n_kb = s_len // kb
    lanes = 2 * Q_TILE
    ones = jnp.ones((16, kb), BF16)
    if diff:
        lp = lam_ref[...]
        lam = (jnp.exp(jnp.sum(lp[0:1] * lp[1:2], axis=-1, keepdims=True))
               - jnp.exp(jnp.sum(lp[2:3] * lp[3:4], axis=-1, keepdims=True)) + lambda_init)

    def scores(i, s_buf):
        h, u = i // n_q, i % n_q
        q2 = qt_ref[0, u, pl.ds(pl.multiple_of(h * LANES, LANES), LANES), :]
        qt = jnp.concatenate([q2[:HEAD_W], q2[HEAD_W:]], axis=1)
        m8 = None
        for j in range(n_kb):
            s = _dot(k_ref[0, h, j * kb:(j + 1) * kb, :], qt)
            s_buf[j * kb:(j + 1) * kb, :] = s
            mj = jnp.max(s.reshape(kb // 8, 8, lanes), axis=0)
            m8 = mj if m8 is None else jnp.maximum(m8, mj)
        return jnp.max(m8, axis=0, keepdims=True)

    def finish(i, s_buf, m):
        h, u = i // n_q, i % n_q
        m8 = jnp.broadcast_to(m, (8, lanes))
        acc = None
        for j in range(n_kb):
            s = s_buf[j * kb:(j + 1) * kb, :]
            p = jnp.exp2(s.reshape(kb // 8, 8, lanes) - m8[None]).reshape(kb, lanes).astype(BF16)
            vt = vt_ref[0, pl.ds(pl.multiple_of(h * HEAD_W, HEAD_W), HEAD_W), j * kb:(j + 1) * kb]
            part = _dot(jnp.concatenate([vt, ones], axis=0), p)
            acc = part if acc is None else acc + part
        o = acc[:HEAD_W] / acc[HEAD_W:HEAD_W + 1]
        if diff:
            od = o[:, :Q_TILE] - lam * o[:, Q_TILE:]
            ms = jnp.mean(od * od, axis=0, keepdims=True)
            y = od * lax.rsqrt(ms + EPS) * gain_ref[...] * (1.0 - lambda_init)
            o_ref[0, u, pl.ds(pl.multiple_of(h * HEAD_W, HEAD_W), HEAD_W), :] = y.astype(o_ref.dtype)
        else:
            y = jnp.concatenate([o[:, :Q_TILE], o[:, Q_TILE:]], axis=0)
            o_ref[0, u, pl.ds(pl.multiple_of(h * LANES, LANES), LANES), :] = y.astype(o_ref.dtype)

    n_units = n_kv * n_q
    assert n_units % 2 == 0

    def two_stages(t, m_even):
        m_odd = scores(2 * t + 1, s_odd)
        finish(2 * t, s_even, m_even)
        m_next = scores(2 * t + 2, s_even)
        finish(2 * t + 1, s_odd, m_odd)
        return m_next

    m_even = lax.fori_loop(0, n_units // 2 - 1, two_stages, scores(0, s_even))
    m_odd = scores(n_units - 1, s_odd)
    finish(n_units - 2, s_even, m_even)
    finish(n_units - 1, s_odd, m_odd)


def _attention(qt, k, vt, lam_p, gain, *, diff, lambda_init):
    b, n_q, qrows, _ = qt.shape
    n_kv, s = k.shape[1], k.shape[2]
    assert qrows == n_kv * LANES and vt.shape[1] == n_kv * HEAD_W
    kb = KEY_BLOCK if s % KEY_BLOCK == 0 else s
    kern = functools.partial(_attn_kernel, n_kv=n_kv, n_q=n_q, kb=kb, diff=diff, lambda_init=lambda_init)
    whole = lambda shape: pl.BlockSpec((1,) + shape[1:], lambda i: (i,) + (0,) * (len(shape) - 1),
                                       pipeline_mode=pl.Buffered(1))
    return pl.pallas_call(
        kern,
        out_shape=jax.ShapeDtypeStruct((b, n_q, 256, Q_TILE), BF16),
        grid=(b,),
        in_specs=[whole(qt.shape), whole(k.shape), whole(vt.shape),
                  pl.BlockSpec(lam_p.shape, lambda i: (0, 0)),
                  pl.BlockSpec(gain.shape, lambda i: (0, 0))],
        out_specs=pl.BlockSpec((1, n_q, 256, Q_TILE), lambda i: (i, 0, 0, 0)),
        scratch_shapes=[pltpu.VMEM((s, 2 * Q_TILE), F32), pltpu.VMEM((s, 2 * Q_TILE), F32)],
        compiler_params=pltpu.CompilerParams(
            dimension_semantics=("parallel",), vmem_limit_bytes=VMEM_LIMIT_BYTES),
        name="diff_attn" if diff else "gqa_attn",
    )(qt, k, vt, lam_p, gain)


def _retention_kernel(q_ref, k_ref, v_ref, s0_ref, dec_ref, o_ref, sn_ref, sb_scr, *, chunk, n_chunks):
    c_len = chunk
    dec = dec_ref[...]
    lg = jnp.minimum(dec, 0.0) - jnp.log1p(jnp.exp(-jnp.abs(dec)))
    row = lax.broadcasted_iota(jnp.int32, (c_len, c_len), 0)
    col = lax.broadcasted_iota(jnp.int32, (c_len, c_len), 1)
    rel = (row - col).astype(F32)
    pos = lax.broadcasted_iota(jnp.int32, (c_len, HEAD_W), 0).astype(F32)
    masks, qdf, qdb, kdf, kdb, cdf, cdb = [], [], [], [], [], [], []
    for h in range(RET_HEADS):
        lf = lg[0:1, h:h + 1]
        lb = lg[1:2, h:h + 1]
        masks.append(jnp.where(rel >= 0, jnp.exp(lf * jnp.maximum(rel, 0.0)), 0.0)
                     + jnp.where(rel <= 0, jnp.exp(lb * jnp.maximum(-rel, 0.0)), 0.0))
        qdf.append(jnp.exp(lf * (pos + 1.0)))
        qdb.append(jnp.exp(lb * (c_len - pos)))
        kdf.append(jnp.exp(lf * (c_len - 1.0 - pos)))
        kdb.append(jnp.exp(lb * pos))
        cdf.append(jnp.exp(lf * c_len))
        cdb.append(jnp.exp(lb * c_len))

    def chunk_of(ref, c, h):
        start = pl.multiple_of(c * c_len, c_len)
        return ref[0, pl.ds(start, c_len), h * HEAD_W:(h + 1) * HEAD_W]

    def kv_state(kc, vc, kdecay):
        kd = (kc.astype(F32) * kdecay).astype(BF16)
        return lax.dot_general(kd, vc, (((0,), (0,)), ((), ())), preferred_element_type=F32)

    def bwd_body(i, states):
        c = n_chunks - 1 - i
        new = []
        for h in range(RET_HEADS):
            sb_scr[c, h] = states[h]
            new.append(states[h] * cdb[h] + kv_state(chunk_of(k_ref, c, h), chunk_of(v_ref, c, h), kdb[h]))
        return tuple(new)

    sb_fin = lax.fori_loop(0, n_chunks, bwd_body, tuple(s0_ref[0, 1, h] for h in range(RET_HEADS)))

    def fwd_body(c, states):
        new, outs = [], []
        for h in range(RET_HEADS):
            qc, kc, vc = chunk_of(q_ref, c, h), chunk_of(k_ref, c, h), chunk_of(v_ref, c, h)
            a = lax.dot_general(qc, kc, (((1,), (1,)), ((), ())), preferred_element_type=F32) * masks[h]
            o = _dot(a.astype(BF16), vc)
            o = o + _dot(qc, states[h].astype(BF16)) * qdf[h]
            o = o + _dot(qc, sb_scr[c, h].astype(BF16)) * qdb[h]
            ms = jnp.mean(o * o, axis=-1, keepdims=True)
            outs.append(o * lax.rsqrt(ms + EPS))
            new.append(states[h] * cdf[h] + kv_state(kc, vc, kdf[h]))
        start = pl.multiple_of(c * c_len, c_len)
        o_ref[0, pl.ds(start, c_len), :] = jnp.concatenate(outs, axis=-1).astype(o_ref.dtype)
        return tuple(new)

    sf_fin = lax.fori_loop(0, n_chunks, fwd_body, tuple(s0_ref[0, 0, h] for h in range(RET_HEADS)))
    for h in range(RET_HEADS):
        sn_ref[0, 0, h] = sf_fin[h]
        sn_ref[0, 1, h] = sb_fin[h]


def _retention(q, k, v, s0, decay, *, chunk):
    b, t, w = q.shape
    n_chunks = t // chunk
    tok = pl.BlockSpec((1, t, w), lambda i: (i, 0, 0))
    st = pl.BlockSpec((1, 2, RET_HEADS, HEAD_W, HEAD_W), lambda i: (i, 0, 0, 0, 0))
    return pl.pallas_call(
        functools.partial(_retention_kernel, chunk=chunk, n_chunks=n_chunks),
        out_shape=[jax.ShapeDtypeStruct((b, t, w), BF16),
                   jax.ShapeDtypeStruct((b, 2, RET_HEADS, HEAD_W, HEAD_W), F32)],
        grid=(b,),
        in_specs=[tok, tok, tok, st, pl.BlockSpec((2, RET_HEADS), lambda i: (0, 0))],
        out_specs=[tok, st],
        scratch_shapes=[pltpu.VMEM((n_chunks, RET_HEADS, HEAD_W, HEAD_W), F32)],
        compiler_params=pltpu.CompilerParams(
            dimension_semantics=("parallel",), vmem_limit_bytes=VMEM_LIMIT_BYTES),
        name="retention",
    )(q, k, v, s0, decay)


def _merge_kernel(x_ref, mod_ref, ng_ref, oa_ref, prev_ref, next_ref, og_ref, yb_ref, yc_ref, yd_ref,
                  cw_ref, wg_ref, wbr_ref, wo_ref, fg_ref, o_ref, *, tm, last):
    j = pl.program_id(1)
    x = x_ref[0]
    hb = _modulated_norm(x, mod_ref, ng_ref).astype(BF16)

    g = oa_ref[0, :, 256:512].astype(F32)
    row = lax.broadcasted_iota(jnp.int32, g.shape, 0)
    g_first = jnp.where(j == 0, 0.0, prev_ref[0, 7:8, 256:512].astype(F32))
    g_last = jnp.where(j == pl.num_programs(1) - 1, 0.0, next_ref[0, 0:1, 256:512].astype(F32))
    g_prev = jnp.where(row == 0, g_first, pltpu.roll(g, 1, 0))
    g_next = jnp.where(row == tm - 1, g_last, pltpu.roll(g, tm - 1, 0))
    conv = g_prev * cw_ref[0:1, :] + g * cw_ref[1:2, :] + g_next * cw_ref[2:3, :]
    def token_major(ref):
        tiles = [ref[0, c].astype(F32).T for c in range(tm // Q_TILE)]
        return (tiles[0] if len(tiles) == 1 else jnp.concatenate(tiles, axis=0)).astype(BF16)

    ys = [(oa_ref[0, :, 0:256].astype(F32) * conv).astype(BF16),
          token_major(yb_ref) * og_ref[0, :, 0:256],
          yc_ref[0] * og_ref[0, :, 256:512],
          token_major(yd_ref) * og_ref[0, :, 512:768]]

    merged = None
    for i in range(N_BRANCH):
        gate = jax.nn.sigmoid(_dot(hb, wg_ref[:, i * 1024:(i + 1) * 1024]))
        term = gate * _dot(ys[i], wbr_ref[i])
        merged = term if merged is None else merged + term
    out = _dot(merged.astype(BF16), wo_ref[...])
    xn = x + mod_ref[0, 2:3, :] * out
    if last:
        ms = jnp.mean(xn * xn, axis=-1, keepdims=True)
        xn = xn * lax.rsqrt(ms + EPS) * fg_ref[...]
    o_ref[0] = xn


def _merge(x, mod, ng, oa, og, yb, yc, yd, conv_w, wg, wbr, wo, fg, *, tm, last):
    b, t, d = x.shape
    nt = t // tm
    rb = tm // 8
    per_batch_mod = mod.shape[0] > 1
    tok = lambda w: pl.BlockSpec((1, tm, w), lambda i, j: (i, j, 0))
    q_tiles = pl.BlockSpec((1, tm // Q_TILE, 256, Q_TILE), lambda i, j: (i, j, 0, 0))
    const = lambda shape: pl.BlockSpec(shape, lambda i, j: (0,) * len(shape))
    in_specs = [tok(d),
                pl.BlockSpec((1, 3, d), (lambda i, j: (i, 0, 0)) if per_batch_mod else (lambda i, j: (0, 0, 0))),
                const((1, d)), tok(512),
                pl.BlockSpec((1, 8, 512), lambda i, j: (i, jnp.maximum(j * rb - 1, 0), 0)),
                pl.BlockSpec((1, 8, 512), lambda i, j: (i, jnp.minimum((j + 1) * rb, t // 8 - 1), 0)),
                tok(768), q_tiles, tok(256), q_tiles,
                const(conv_w.shape), const(wg.shape), const(wbr.shape), const(wo.shape), const((1, d))]
    return pl.pallas_call(
        functools.partial(_merge_kernel, tm=tm, last=last),
        out_shape=jax.ShapeDtypeStruct((b, t, d), F32),
        grid=(b, nt),
        in_specs=in_specs,
        out_specs=tok(d),
        compiler_params=pltpu.CompilerParams(
            dimension_semantics=("parallel", "parallel"), vmem_limit_bytes=VMEM_LIMIT_BYTES),
        name="merge",
    )(x, mod, ng, oa, oa, oa, og, yb, yc, yd, conv_w, wg, wbr, wo, fg)


def _pack_w_in(w):
    d = w.shape[0]
    dq = w[:, 2816:3072].reshape(d, DIFF_HEADS, 2, 1, DIFF_HEAD_DIM)
    eye = jnp.eye(2, dtype=w.dtype).reshape(1, 1, 2, 2, 1)
    dq = (dq * eye).reshape(d, 2 * 256)
    return jnp.concatenate([w[:, :2816], dq, w[:, 3072:]], axis=1).astype(BF16)


def _rope_tables(n_tokens, head_dim):
    rows = n_tokens // GRID_W
    row = jnp.repeat(jnp.arange(rows, dtype=F32), GRID_W)
    col = jnp.tile(jnp.arange(GRID_W, dtype=F32), rows)
    n_axis = head_dim // 4
    inv_freq = ROPE_THETA ** (-jnp.arange(n_axis, dtype=F32) / n_axis)
    ang = jnp.concatenate([row[:, None] * inv_freq, col[:, None] * inv_freq], axis=-1)
    cos, sin = jnp.cos(ang), jnp.sin(ang)
    reps = LANES // head_dim
    return (jnp.tile(jnp.concatenate([cos, cos], axis=-1), (1, reps)),
            jnp.tile(jnp.concatenate([-sin, sin], axis=-1), (1, reps)))


def _pick(n, pref):
    return pref if n % pref == 0 else n


def kernel(x_prompt, x_sample, cache_gqa_k, cache_gqa_v, cache_diff_k, cache_diff_v, state_ret, c, c_ctx,
           w_ada, b_ada, norm_gain, w_in, conv_w, gqa_q_gain, gqa_k_gain, ret_decay, diff_lambda,
           diff_norm_gain, w_branch, w_mgate, w_out, final_gain):
    depth = w_in.shape[0]
    bc, tc, d = x_prompt.shape
    bl, tl, _ = x_sample.shape

    cond = jnp.concatenate([c_ctx[None], c], axis=0)
    n_cond = cond.shape[0]
    cond = jnp.pad(cond, ((0, -n_cond % 8), (0, 0)))
    mod = _adaln(cond, w_ada.astype(BF16), b_ada)
    mod = mod.reshape(depth, cond.shape[0], 3, d)

    rope = _rope_tables(tl, HEAD_W) + _rope_tables(tl, DIFF_HEAD_DIM)
    fg = final_gain.reshape(1, d)
    zero_state = jnp.zeros((bc, 2, RET_HEADS, HEAD_W, HEAD_W), F32)

    def to_heads(cache):
        b, p = cache.shape[:2]
        return jnp.transpose(cache.reshape(b, p, -1, HEAD_W), (0, 2, 1, 3)).astype(BF16)

    def to_features(cache):
        b, p = cache.shape[:2]
        return jnp.transpose(cache.reshape(b, p, -1), (0, 2, 1)).astype(BF16)

    xp, xs = x_prompt, x_sample
    new_caches = ([], [], [], [], [])
    for l in range(depth):
        lambda_init = 0.8 - 0.6 * math.exp(-0.3 * l)
        w_packed = _pack_w_in(w_in[l])
        ng = norm_gain[l].reshape(1, d)
        qg = jnp.tile(gqa_q_gain[l], 256 // HEAD_W).reshape(1, 256)
        kg = jnp.tile(gqa_k_gain[l], 128 // HEAD_W).reshape(1, 128)
        dgain = diff_norm_gain[l].reshape(HEAD_W, 1)
        wg, wbr, wo = w_mgate[l].astype(BF16), w_branch[l].astype(BF16), w_out[l].astype(BF16)

        for latent in (False, True):
            x = xs if latent else xp
            b, t, _ = x.shape
            m = mod[l, 1:1 + bl] if latent else mod[l, 0:1]
            tm = _pick(t, 512) if latent else _pick(t, 256)
            outs = _inproj(x, m, ng, w_packed, qg, kg, rope, latent=latent, tm=tm)
            oa, og, qb, kb, vb, qc, kc, vc, qd, kd, vd = outs[:11]
            if latent:
                kb = jnp.concatenate([to_heads(cache_gqa_k[:, l]), kb], axis=2)
                vb = jnp.concatenate([to_features(cache_gqa_v[:, l]), vb], axis=2)
                kd = jnp.concatenate([to_heads(cache_diff_k[:, l]), kd], axis=2)
                vd = jnp.concatenate([to_features(cache_diff_v[:, l]), vd], axis=2)
                s0 = state_ret[:, l]
            else:
                for lst, a in zip(new_caches[:4], outs[11:]):
                    lst.append(a)
                s0 = zero_state
            yb = _attention(qb, kb, vb, diff_lambda[l], dgain, diff=False, lambda_init=lambda_init)
            yd = _attention(qd, kd, vd, diff_lambda[l], dgain, diff=True, lambda_init=lambda_init)
            yc, s_new = _retention(qc, kc, vc, s0, ret_decay[l], chunk=_pick(t, 256))
            if not latent:
                new_caches[4].append(s_new)
            x = _merge(x, m, ng, oa, og, yb, yc, yd, conv_w[l], wg, wbr, wo, fg, tm=tm, last=(l == depth - 1))
            if latent:
                xs = x
            else:
                xp = x

    ck = jnp.stack(new_caches[0], axis=1).reshape(bc, depth, tc, GQA_KV_HEADS, HEAD_W)
    cv = jnp.stack(new_caches[1], axis=1).reshape(bc, depth, tc, GQA_KV_HEADS, HEAD_W)
    cdk = jnp.stack(new_caches[2], axis=1).reshape(bc, depth, tc, DIFF_HEADS, 2, DIFF_HEAD_DIM)
    cdv = jnp.stack(new_caches[3], axis=1).reshape(bc, depth, tc, DIFF_HEADS, 2 * DIFF_HEAD_DIM)
    cs = jnp.stack(new_caches[4], axis=1)
    return (xp, xs, ck, cv, cdk, cdv, cs)
```

```python
import functools
import math

import jax
import jax.numpy as jnp
from jax import lax
from jax.experimental import pallas as pl
from jax.experimental.pallas import tpu as pltpu

F32 = jnp.float32
BF16 = jnp.bfloat16

GRID_W = 64
BRANCH_W = 256
HEAD_W = 64
GQA_KV_HEADS = 2
RET_HEADS = 4
DIFF_HEADS = 4
DIFF_HEAD_DIM = 32
ROPE_THETA = 10000.0
EPS = 1e-6
LOG2_E = math.log2(math.e)
N_BRANCH = 4
LANES = 128
Q_TILE = 256
KEY_BLOCK = 512
VMEM_LIMIT_BYTES = 56 * 1024 * 1024

COL_A = 0
COL_B = 1024
COL_C = 1792
COL_D = 2816
W_IN_PACKED = 4096


def _vmem_bytes(shape, dtype):
    itemsize = jnp.dtype(dtype).itemsize
    sublanes = 8 * (4 // itemsize)
    rows = -(-shape[-2] // sublanes) * sublanes
    cols = -(-shape[-1] // LANES) * LANES
    return math.prod(shape[:-2]) * rows * cols * itemsize


def _silu(x):
    return x * jax.nn.sigmoid(x)


def _dot(a, b):
    return jnp.dot(a, b, preferred_element_type=F32)


def _group_ones(width):
    r = lax.broadcasted_iota(jnp.int32, (width, width), 0) >> 6
    c = lax.broadcasted_iota(jnp.int32, (width, width), 1) >> 6
    return jnp.where(r == c, 1.0, 0.0).astype(BF16)


def _head_rms(z, ones):
    zz = z * z
    hi = zz.astype(BF16)
    lo = (zz - hi.astype(F32)).astype(BF16)
    ss = _dot(hi, ones) + _dot(lo, ones)
    return z * lax.rsqrt(ss * (1.0 / HEAD_W) + EPS)


def _rope(x, cos_t, sin_t, half):
    lane = lax.broadcasted_iota(jnp.int32, (x.shape[0], LANES), 1)
    first = (lane & (2 * half - 1)) < half
    outs = []
    for j in range(x.shape[1] // LANES):
        xc = x[:, j * LANES:(j + 1) * LANES]
        partner = jnp.where(first, pltpu.roll(xc, LANES - half, 1), pltpu.roll(xc, half, 1))
        outs.append(xc * cos_t + partner * sin_t)
    return outs[0] if len(outs) == 1 else jnp.concatenate(outs, axis=1)


def _modulated_norm(x, mod_ref, ng_ref):
    ms = jnp.mean(x * x, axis=-1, keepdims=True)
    y = x * lax.rsqrt(ms + EPS) * ng_ref[...]
    return y * (1.0 + mod_ref[0, 1:2, :]) + mod_ref[0, 0:1, :]


def _adaln_kernel(cond_ref, w_ref, b_ref, o_ref):
    a = _silu(cond_ref[...]).astype(BF16)
    o_ref[0] = _dot(a, w_ref[0]) + b_ref[0]


def _adaln(cond, w_ada, b_ada):
    depth, d, d3 = w_ada.shape
    rows = cond.shape[0]
    return pl.pallas_call(
        _adaln_kernel,
        out_shape=jax.ShapeDtypeStruct((depth, rows, d3), F32),
        grid=(depth,),
        in_specs=[pl.BlockSpec((rows, d), lambda l: (0, 0)),
                  pl.BlockSpec((1, d, d3), lambda l: (l, 0, 0)),
                  pl.BlockSpec((1, 1, d3), lambda l: (l, 0, 0))],
        out_specs=pl.BlockSpec((1, rows, d3), lambda l: (l, 0, 0)),
        compiler_params=pltpu.CompilerParams(vmem_limit_bytes=VMEM_LIMIT_BYTES),
        name="adaln",
    )(cond, w_ada, b_ada.reshape(depth, 1, d3))


def _store_q_tiles(ref, q):
    qt = q.T.astype(BF16)
    for c in range(qt.shape[1] // Q_TILE):
        ref[0, c] = qt[:, c * Q_TILE:(c + 1) * Q_TILE]


def _inproj_kernel(*refs, latent):
    if latent:
        (x_ref, mod_ref, ng_ref, w_ref, qg_ref, kg_ref, c64_ref, s64_ref, c32_ref, s32_ref,
         oa_ref, og_ref, qb_ref, kb_ref, vb_ref, qc_ref, kc_ref, vc_ref, qd_ref, kd_ref, vd_ref) = refs
    else:
        (x_ref, mod_ref, ng_ref, w_ref, qg_ref, kg_ref,
         oa_ref, og_ref, qb_ref, kb_ref, vb_ref, qc_ref, kc_ref, vc_ref, qd_ref, kd_ref, vd_ref,
         ck_ref, cv_ref, cdk_ref, cdv_ref) = refs
    hb = _modulated_norm(x_ref[0], mod_ref, ng_ref).astype(BF16)

    z = _dot(hb, w_ref[:, COL_A:COL_A + 1024])
    oa_ref[0, :, 0:256] = (z[:, 0:256] * _silu(z[:, 768:1024])).astype(BF16)
    oa_ref[0, :, 256:512] = (z[:, 256:512] * z[:, 512:768]).astype(BF16)

    z = _dot(hb, w_ref[:, COL_B:COL_B + 768])
    q = _head_rms(z[:, 0:256], _group_ones(256)) * qg_ref[...]
    k = _head_rms(z[:, 256:384], _group_ones(128)) * kg_ref[...]
    v = z[:, 384:512]
    if latent:
        q = _rope(q, c64_ref[...], s64_ref[...], 32)
        k_att = _rope(k, c64_ref[...], s64_ref[...], 32)
    else:
        k_att = k
        ck_ref[0] = k
        cv_ref[0] = v
    _store_q_tiles(qb_ref, q * (HEAD_W ** -0.5 * LOG2_E))
    vb_ref[0] = v.T.astype(BF16)
    for h in range(GQA_KV_HEADS):
        kb_ref[0, h] = k_att[:, h * HEAD_W:(h + 1) * HEAD_W].astype(BF16)
    og_ref[0, :, 0:256] = _silu(z[:, 512:768]).astype(BF16)

    z = _dot(hb, w_ref[:, COL_C:COL_C + 1024])
    qc_ref[0] = z[:, 0:256].astype(BF16)
    _store_q_tiles(kc_ref, z[:, 256:512] * (HEAD_W ** -0.5))
    vc_ref[0] = z[:, 512:768].astype(BF16)
    og_ref[0, :, 256:512] = _silu(z[:, 768:1024]).astype(BF16)

    z = _dot(hb, w_ref[:, COL_D:COL_D + 1280])
    q = z[:, 0:512]
    k = z[:, 512:768]
    v = z[:, 768:1024]
    if latent:
        q = _rope(q, c32_ref[...], s32_ref[...], 16)
        k_att = _rope(k, c32_ref[...], s32_ref[...], 16)
    else:
        k_att = k
        cdk_ref[0] = k
        cdv_ref[0] = v
    _store_q_tiles(qd_ref, q * (DIFF_HEAD_DIM ** -0.5 * LOG2_E))
    vd_ref[0] = v.T.astype(BF16)
    for h in range(DIFF_HEADS):
        kd_ref[0, h] = k_att[:, h * HEAD_W:(h + 1) * HEAD_W].astype(BF16)
    og_ref[0, :, 512:768] = _silu(z[:, 1024:1280]).astype(BF16)


def _inproj(x, mod, ng, w_packed, qg, kg, rope_tabs, *, latent, tm):
    b, t, d = x.shape
    nt = t // tm
    per_batch_mod = mod.shape[0] > 1
    tok = lambda w: pl.BlockSpec((1, tm, w), lambda i, j: (i, j, 0))
    tok_t = lambda w: pl.BlockSpec((1, w, tm), lambda i, j: (i, 0, j))
    heads = lambda n: pl.BlockSpec((1, n, tm, HEAD_W), lambda i, j: (i, 0, j, 0))
    const = lambda shape: pl.BlockSpec(shape, lambda i, j: (0,) * len(shape))
    in_specs = [tok(d),
                pl.BlockSpec((1, 3, d), (lambda i, j: (i, 0, 0)) if per_batch_mod else (lambda i, j: (0, 0, 0))),
                const((1, d)), const(w_packed.shape), const((1, 256)), const((1, 128))]
    args = [x, mod, ng, w_packed, qg, kg]
    if latent:
        in_specs += [pl.BlockSpec((tm, LANES), lambda i, j: (j, 0))] * 4
        args += list(rope_tabs)
    sd = jax.ShapeDtypeStruct
    qn = tm // Q_TILE
    q_tiles = lambda w: pl.BlockSpec((1, qn, w, Q_TILE), lambda i, j: (i, j, 0, 0))
    out_shape = [sd((b, t, 512), BF16), sd((b, t, 768), BF16),
                 sd((b, t // Q_TILE, 256, Q_TILE), BF16), sd((b, GQA_KV_HEADS, t, HEAD_W), BF16), sd((b, 128, t), BF16),
                 sd((b, t, 256), BF16), sd((b, t // Q_TILE, 256, Q_TILE), BF16), sd((b, t, 256), BF16),
                 sd((b, t // Q_TILE, 512, Q_TILE), BF16), sd((b, DIFF_HEADS, t, HEAD_W), BF16), sd((b, 256, t), BF16)]
    out_specs = [tok(512), tok(768), q_tiles(256), heads(GQA_KV_HEADS), tok_t(128),
                 tok(256), q_tiles(256), tok(256), q_tiles(512), heads(DIFF_HEADS), tok_t(256)]
    if not latent:
        out_shape += [sd((b, t, 128), F32), sd((b, t, 128), F32), sd((b, t, 256), F32), sd((b, t, 256), F32)]
        out_specs += [tok(128), tok(128), tok(256), tok(256)]
    return pl.pallas_call(
        functools.partial(_inproj_kernel, latent=latent),
        out_shape=out_shape,
        grid=(b, nt),
        in_specs=in_specs,
        out_specs=out_specs,
        compiler_params=pltpu.CompilerParams(
            dimension_semantics=("parallel", "parallel"), vmem_limit_bytes=VMEM_LIMIT_BYTES),
        name="inproj_lat" if latent else "inproj_ctx",
    )(*args)


def _attn_kernel(*refs, n_kv, n_q, kb, has_cache, diff, lambda_init):
    if has_cache:
        qt_ref, kc_ref, vc_ref, kn_ref, vn_ref, lam_ref, gain_ref, o_ref, s_even, s_odd = refs
        sources = [(kc_ref, vc_ref), (kn_ref, vn_ref)]
    else:
        qt_ref, kn_ref, vn_ref, lam_ref, gain_ref, o_ref, s_even, s_odd = refs
        sources = [(kn_ref, vn_ref)]
    blocks = [(k_ref, v_ref, a) for k_ref, v_ref in sources for a in range(0, k_ref.shape[2], kb)]
    lanes = 2 * Q_TILE
    ones = jnp.ones((16, kb), BF16)
    if diff:
        lp = lam_ref[...]
        lam = (jnp.exp(jnp.sum(lp[0:1] * lp[1:2], axis=-1, keepdims=True))
               - jnp.exp(jnp.sum(lp[2:3] * lp[3:4], axis=-1, keepdims=True)) + lambda_init)

    def stage(i_new, s_new, i_old, s_old, m_old):
        if i_new is not None:
            h_new, u_new = i_new // n_q, i_new % n_q
            q2 = qt_ref[0, u_new, pl.ds(pl.multiple_of(h_new * LANES, LANES), LANES), :]
            qt = jnp.concatenate([q2[:HEAD_W], q2[HEAD_W:]], axis=1)
        if i_old is not None:
            h_old, u_old = i_old // n_q, i_old % n_q
            v_rows = pl.ds(pl.multiple_of(h_old * HEAD_W, HEAD_W), HEAD_W)
            m8_old = jnp.broadcast_to(m_old, (8, lanes))
        m8, acc = None, None
        for j, (k_ref, v_ref, a) in enumerate(blocks):
            if i_new is not None:
                s = _dot(k_ref[0, h_new, a:a + kb, :], qt)
                s_new[j * kb:(j + 1) * kb, :] = s
                mj = jnp.max(s.reshape(kb // 8, 8, lanes), axis=0)
                m8 = mj if m8 is None else jnp.maximum(m8, mj)
            if i_old is not None:
                s = s_old[j * kb:(j + 1) * kb, :]
                p = jnp.exp2(s.reshape(kb // 8, 8, lanes) - m8_old[None]).reshape(kb, lanes).astype(BF16)
                vt = jnp.concatenate([v_ref[0, v_rows, a:a + kb], ones], axis=0)
                part = _dot(vt, p)
                acc = part if acc is None else acc + part
        if i_old is not None:
            o = acc[:HEAD_W] / acc[HEAD_W:HEAD_W + 1]
            if diff:
                od = o[:, :Q_TILE] - lam * o[:, Q_TILE:]
                ms = jnp.mean(od * od, axis=0, keepdims=True)
                y = od * lax.rsqrt(ms + EPS) * gain_ref[...] * (1.0 - lambda_init)
                o_ref[0, u_old, pl.ds(pl.multiple_of(h_old * HEAD_W, HEAD_W), HEAD_W), :] = y.astype(o_ref.dtype)
            else:
                y = jnp.concatenate([o[:, :Q_TILE], o[:, Q_TILE:]], axis=0)
                o_ref[0, u_old, pl.ds(pl.multiple_of(h_old * LANES, LANES), LANES), :] = y.astype(o_ref.dtype)
        return None if i_new is None else jnp.max(m8, axis=0, keepdims=True)

    n_units = n_kv * n_q
    assert n_units % 2 == 0

    def two_stages(t, m_even):
        m_odd = stage(2 * t + 1, s_odd, 2 * t, s_even, m_even)
        return stage(2 * t + 2, s_even, 2 * t + 1, s_odd, m_odd)

    m_even = lax.fori_loop(0, n_units // 2 - 1, two_stages, stage(0, s_even, None, None, None))
    m_odd = stage(n_units - 1, s_odd, n_units - 2, s_even, m_even)
    stage(None, None, n_units - 1, s_odd, m_odd)


def _attention(qt, k_parts, vt_parts, lam_p, gain, *, diff, lambda_init):
    b, n_q, qrows, _ = qt.shape
    n_kv = k_parts[0].shape[1]
    lens = [k.shape[2] for k in k_parts]
    assert qrows == n_kv * LANES and all(v.shape[1] == n_kv * HEAD_W for v in vt_parts)
    kb = KEY_BLOCK if all(n % KEY_BLOCK == 0 for n in lens) else math.gcd(*lens)
    kern = functools.partial(_attn_kernel, n_kv=n_kv, n_q=n_q, kb=kb, has_cache=len(k_parts) == 2, diff=diff,
                             lambda_init=lambda_init)
    kv = [a for pair in zip(k_parts, vt_parts) for a in pair]
    score_bytes = 2 * sum(lens) * 2 * Q_TILE * 4
    out_bytes = 2 * _vmem_bytes((n_q, 256, Q_TILE), BF16)
    operand_bytes = sum(_vmem_bytes(a.shape[1:], a.dtype) for a in [qt] + kv)
    double_buffered = score_bytes + out_bytes + 2 * operand_bytes <= VMEM_LIMIT_BYTES
    mode = {} if double_buffered else dict(pipeline_mode=pl.Buffered(1))
    whole = lambda a: pl.BlockSpec((1,) + a.shape[1:], lambda i: (i,) + (0,) * (a.ndim - 1), **mode)
    return pl.pallas_call(
        kern,
        out_shape=jax.ShapeDtypeStruct((b, n_q, 256, Q_TILE), BF16),
        grid=(b,),
        in_specs=[whole(qt)] + [whole(a) for a in kv] + [pl.BlockSpec(lam_p.shape, lambda i: (0, 0)),
                                                          pl.BlockSpec(gain.shape, lambda i: (0, 0))],
        out_specs=pl.BlockSpec((1, n_q, 256, Q_TILE), lambda i: (i, 0, 0, 0)),
        scratch_shapes=[pltpu.VMEM((sum(lens), 2 * Q_TILE), F32), pltpu.VMEM((sum(lens), 2 * Q_TILE), F32)],
        compiler_params=pltpu.CompilerParams(
            dimension_semantics=("parallel",), vmem_limit_bytes=VMEM_LIMIT_BYTES),
        name="diff_attn" if diff else "gqa_attn",
    )(qt, *kv, lam_p, gain)


def _retention_kernel(q_ref, kt_ref, v_ref, s0_ref, dec_ref, o_ref, sn_ref, kv_scr, sf_scr, sb_scr, *, n_chunks):
    c_len = Q_TILE
    dec = dec_ref[...]
    lg = jnp.minimum(dec, 0.0) - jnp.log1p(jnp.exp(-jnp.abs(dec)))
    row = lax.broadcasted_iota(jnp.int32, (c_len, c_len), 0)
    col = lax.broadcasted_iota(jnp.int32, (c_len, c_len), 1)
    rel = (row - col).astype(F32)
    pos = lax.broadcasted_iota(jnp.int32, (c_len, HEAD_W), 0).astype(F32)
    pos_t = lax.broadcasted_iota(jnp.int32, (HEAD_W, c_len), 1).astype(F32)
    masks, qdf, qdb, kdf, kdb, cdf, cdb = [], [], [], [], [], [], []
    for h in range(RET_HEADS):
        lf = lg[0:1, h:h + 1]
        lb = lg[1:2, h:h + 1]
        masks.append(jnp.where(rel >= 0, jnp.exp(lf * jnp.maximum(rel, 0.0)), 0.0)
                     + jnp.where(rel <= 0, jnp.exp(lb * jnp.maximum(-rel, 0.0)), 0.0))
        qdf.append(jnp.exp(lf * (pos + 1.0)))
        qdb.append(jnp.exp(lb * (c_len - pos)))
        kdf.append(jnp.exp(lf * (c_len - 1.0 - pos_t)))
        kdb.append(jnp.exp(lb * pos_t))
        cdf.append(jnp.exp(lf * c_len))
        cdb.append(jnp.exp(lb * c_len))

    def chunk_of(ref, c, h):
        start = pl.multiple_of(c * c_len, c_len)
        return ref[0, pl.ds(start, c_len), h * HEAD_W:(h + 1) * HEAD_W]

    unroll = 2 if n_chunks % 2 == 0 else 1

    def kv_body(c, carry):
        for h in range(RET_HEADS):
            kt = kt_ref[0, c, h * HEAD_W:(h + 1) * HEAD_W, :].astype(F32)
            vc = chunk_of(v_ref, c, h)
            kv_scr[c, h, 0] = _dot((kt * kdf[h]).astype(BF16), vc)
            kv_scr[c, h, 1] = _dot((kt * kdb[h]).astype(BF16), vc)
        return carry

    lax.fori_loop(0, n_chunks, kv_body, 0, unroll=unroll)

    def scan_body(i, states):
        sf, sb = states
        cb = n_chunks - 1 - i
        new_f, new_b = [], []
        for h in range(RET_HEADS):
            sf_scr[i, h] = sf[h]
            sb_scr[cb, h] = sb[h]
            new_f.append(sf[h] * cdf[h] + kv_scr[i, h, 0])
            new_b.append(sb[h] * cdb[h] + kv_scr[cb, h, 1])
        return tuple(new_f), tuple(new_b)

    sf_fin, sb_fin = lax.fori_loop(0, n_chunks, scan_body,
                                   (tuple(s0_ref[0, 0, h] for h in range(RET_HEADS)),
                                    tuple(s0_ref[0, 1, h] for h in range(RET_HEADS))))
    for h in range(RET_HEADS):
        sn_ref[0, 0, h] = sf_fin[h]
        sn_ref[0, 1, h] = sb_fin[h]

    def out_body(c, carry):
        outs = []
        for h in range(RET_HEADS):
            qc, vc = chunk_of(q_ref, c, h), chunk_of(v_ref, c, h)
            a = _dot(qc, kt_ref[0, c, h * HEAD_W:(h + 1) * HEAD_W, :]) * masks[h]
            o = _dot(a.astype(BF16), vc)
            o = o + _dot(qc, sf_scr[c, h].astype(BF16)) * qdf[h]
            o = o + _dot(qc, sb_scr[c, h].astype(BF16)) * qdb[h]
            ms = jnp.mean(o * o, axis=-1, keepdims=True)
            outs.append(o * lax.rsqrt(ms + EPS))
        start = pl.multiple_of(c * c_len, c_len)
        o_ref[0, pl.ds(start, c_len), :] = jnp.concatenate(outs, axis=-1).astype(o_ref.dtype)
        return carry

    lax.fori_loop(0, n_chunks, out_body, 0, unroll=unroll)


def _retention(q, kt, v, s0, decay):
    b, t, w = q.shape
    n_chunks = kt.shape[1]
    assert n_chunks * Q_TILE == t
    tok = pl.BlockSpec((1, t, w), lambda i: (i, 0, 0))
    st = pl.BlockSpec((1, 2, RET_HEADS, HEAD_W, HEAD_W), lambda i: (i, 0, 0, 0, 0))
    state_scr = pltpu.VMEM((n_chunks, RET_HEADS, HEAD_W, HEAD_W), F32)
    return pl.pallas_call(
        functools.partial(_retention_kernel, n_chunks=n_chunks),
        out_shape=[jax.ShapeDtypeStruct((b, t, w), BF16),
                   jax.ShapeDtypeStruct((b, 2, RET_HEADS, HEAD_W, HEAD_W), F32)],
        grid=(b,),
        in_specs=[tok, pl.BlockSpec((1, n_chunks, w, Q_TILE), lambda i: (i, 0, 0, 0)), tok, st,
                  pl.BlockSpec((2, RET_HEADS), lambda i: (0, 0))],
        out_specs=[tok, st],
        scratch_shapes=[pltpu.VMEM((n_chunks, RET_HEADS, 2, HEAD_W, HEAD_W), F32), state_scr, state_scr],
        compiler_params=pltpu.CompilerParams(
            dimension_semantics=("parallel",), vmem_limit_bytes=VMEM_LIMIT_BYTES),
        name="retention",
    )(q, kt, v, s0, decay)


def _merge_kernel(x_ref, mod_ref, ng_ref, oa_ref, prev_ref, next_ref, og_ref, yb_ref, yc_ref, yd_ref,
                  cw_ref, wg_ref, wbr_ref, wo_ref, fg_ref, o_ref, *, tm, last):
    j = pl.program_id(1)
    x = x_ref[0]
    hb = _modulated_norm(x, mod_ref, ng_ref).astype(BF16)

    g = oa_ref[0, :, 256:512].astype(F32)
    row = lax.broadcasted_iota(jnp.int32, g.shape, 0)
    g_first = jnp.where(j == 0, 0.0, prev_ref[0, 7:8, 256:512].astype(F32))
    g_last = jnp.where(j == pl.num_programs(1) - 1, 0.0, next_ref[0, 0:1, 256:512].astype(F32))
    g_prev = jnp.where(row == 0, g_first, pltpu.roll(g, 1, 0))
    g_next = jnp.where(row == tm - 1, g_last, pltpu.roll(g, tm - 1, 0))
    conv = g_prev * cw_ref[0:1, :] + g * cw_ref[1:2, :] + g_next * cw_ref[2:3, :]
    def token_major(ref):
        tiles = [ref[0, c].astype(F32).T for c in range(tm // Q_TILE)]
        return (tiles[0] if len(tiles) == 1 else jnp.concatenate(tiles, axis=0)).astype(BF16)

    ys = [(oa_ref[0, :, 0:256].astype(F32) * conv).astype(BF16),
          token_major(yb_ref) * og_ref[0, :, 0:256],
          yc_ref[0] * og_ref[0, :, 256:512],
          token_major(yd_ref) * og_ref[0, :, 512:768]]

    merged = None
    for i in range(N_BRANCH):
        gate = jax.nn.sigmoid(_dot(hb, wg_ref[:, i * 1024:(i + 1) * 1024]))
        term = gate * _dot(ys[i], wbr_ref[i])
        merged = term if merged is None else merged + term
    out = _dot(merged.astype(BF16), wo_ref[...])
    xn = x + mod_ref[0, 2:3, :] * out
    if last:
        ms = jnp.mean(xn * xn, axis=-1, keepdims=True)
        xn = xn * lax.rsqrt(ms + EPS) * fg_ref[...]
    o_ref[0] = xn


def _merge(x, mod, ng, oa, og, yb, yc, yd, conv_w, wg, wbr, wo, fg, *, tm, last):
    b, t, d = x.shape
    nt = t // tm
    rb = tm // 8
    per_batch_mod = mod.shape[0] > 1
    tok = lambda w: pl.BlockSpec((1, tm, w), lambda i, j: (i, j, 0))
    q_tiles = pl.BlockSpec((1, tm // Q_TILE, 256, Q_TILE), lambda i, j: (i, j, 0, 0))
    const = lambda shape: pl.BlockSpec(shape, lambda i, j: (0,) * len(shape))
    in_specs = [tok(d),
                pl.BlockSpec((1, 3, d), (lambda i, j: (i, 0, 0)) if per_batch_mod else (lambda i, j: (0, 0, 0))),
                const((1, d)), tok(512),
                pl.BlockSpec((1, 8, 512), lambda i, j: (i, jnp.maximum(j * rb - 1, 0), 0)),
                pl.BlockSpec((1, 8, 512), lambda i, j: (i, jnp.minimum((j + 1) * rb, t // 8 - 1), 0)),
                tok(768), q_tiles, tok(256), q_tiles,
                const(conv_w.shape), const(wg.shape), const(wbr.shape), const(wo.shape), const((1, d))]
    return pl.pallas_call(
        functools.partial(_merge_kernel, tm=tm, last=last),
        out_shape=jax.ShapeDtypeStruct((b, t, d), F32),
        grid=(b, nt),
        in_specs=in_specs,
        out_specs=tok(d),
        compiler_params=pltpu.CompilerParams(
            dimension_semantics=("parallel", "parallel"), vmem_limit_bytes=VMEM_LIMIT_BYTES),
        name="merge",
    )(x, mod, ng, oa, oa, oa, og, yb, yc, yd, conv_w, wg, wbr, wo, fg)


def _pack_w_in(w):
    d = w.shape[0]
    dq = w[:, 2816:3072].reshape(d, DIFF_HEADS, 2, 1, DIFF_HEAD_DIM)
    eye = jnp.eye(2, dtype=w.dtype).reshape(1, 1, 2, 2, 1)
    dq = (dq * eye).reshape(d, 2 * 256)
    return jnp.concatenate([w[:, :2816], dq, w[:, 3072:]], axis=1).astype(BF16)


def _rope_tables(n_tokens, head_dim):
    rows = n_tokens // GRID_W
    row = jnp.repeat(jnp.arange(rows, dtype=F32), GRID_W)
    col = jnp.tile(jnp.arange(GRID_W, dtype=F32), rows)
    n_axis = head_dim // 4
    inv_freq = ROPE_THETA ** (-jnp.arange(n_axis, dtype=F32) / n_axis)
    ang = jnp.concatenate([row[:, None] * inv_freq, col[:, None] * inv_freq], axis=-1)
    cos, sin = jnp.cos(ang), jnp.sin(ang)
    reps = LANES // head_dim
    return (jnp.tile(jnp.concatenate([cos, cos], axis=-1), (1, reps)),
            jnp.tile(jnp.concatenate([-sin, sin], axis=-1), (1, reps)))


def _pick(n, pref):
    return pref if n % pref == 0 else n


def kernel(x_prompt, x_sample, cache_gqa_k, cache_gqa_v, cache_diff_k, cache_diff_v, state_ret, c, c_ctx,
           w_ada, b_ada, norm_gain, w_in, conv_w, gqa_q_gain, gqa_k_gain, ret_decay, diff_lambda,
           diff_norm_gain, w_branch, w_mgate, w_out, final_gain):
    depth = w_in.shape[0]
    bc, tc, d = x_prompt.shape
    bl, tl, _ = x_sample.shape

    cond = jnp.concatenate([c_ctx[None], c], axis=0)
    n_cond = cond.shape[0]
    cond = jnp.pad(cond, ((0, -n_cond % 8), (0, 0)))
    mod = _adaln(cond, w_ada.astype(BF16), b_ada)
    mod = mod.reshape(depth, cond.shape[0], 3, d)

    rope = _rope_tables(tl, HEAD_W) + _rope_tables(tl, DIFF_HEAD_DIM)
    fg = final_gain.reshape(1, d)
    zero_state = jnp.zeros((bc, 2, RET_HEADS, HEAD_W, HEAD_W), F32)

    def to_heads(cache):
        b, p = cache.shape[:2]
        return jnp.transpose(cache.reshape(b, p, -1, HEAD_W), (0, 2, 1, 3)).astype(BF16)

    def to_features(cache):
        b, p = cache.shape[:2]
        return jnp.transpose(cache.reshape(b, p, -1), (0, 2, 1)).astype(BF16)

    xp, xs = x_prompt, x_sample
    new_caches = ([], [], [], [], [])
    for l in range(depth):
        lambda_init = 0.8 - 0.6 * math.exp(-0.3 * l)
        w_packed = _pack_w_in(w_in[l])
        ng = norm_gain[l].reshape(1, d)
        qg = jnp.tile(gqa_q_gain[l], 256 // HEAD_W).reshape(1, 256)
        kg = jnp.tile(gqa_k_gain[l], 128 // HEAD_W).reshape(1, 128)
        dgain = diff_norm_gain[l].reshape(HEAD_W, 1)
        wg, wbr, wo = w_mgate[l].astype(BF16), w_branch[l].astype(BF16), w_out[l].astype(BF16)

        for latent in (False, True):
            x = xs if latent else xp
            b, t, _ = x.shape
            m = mod[l, 1:1 + bl] if latent else mod[l, 0:1]
            tm = _pick(t, 512) if latent else _pick(t, 256)
            outs = _inproj(x, m, ng, w_packed, qg, kg, rope, latent=latent, tm=tm)
            oa, og, qb, kb, vb, qc, kc, vc, qd, kd, vd = outs[:11]
            kb, vb, kd, vd = [kb], [vb], [kd], [vd]
            if latent:
                kb.insert(0, to_heads(cache_gqa_k[:, l]))
                vb.insert(0, to_features(cache_gqa_v[:, l]))
                kd.insert(0, to_heads(cache_diff_k[:, l]))
                vd.insert(0, to_features(cache_diff_v[:, l]))
                s0 = state_ret[:, l]
            else:
                for lst, a in zip(new_caches[:4], outs[11:]):
                    lst.append(a)
                s0 = zero_state
            yb = _attention(qb, kb, vb, diff_lambda[l], dgain, diff=False, lambda_init=lambda_init)
            yd = _attention(qd, kd, vd, diff_lambda[l], dgain, diff=True, lambda_init=lambda_init)
            yc, s_new = _retention(qc, kc, vc, s0, ret_decay[l])
            if not latent:
                new_caches[4].append(s_new)
            x = _merge(x, m, ng, oa, og, yb, yc, yd, conv_w[l], wg, wbr, wo, fg, tm=tm, last=(l == depth - 1))
            if latent:
                xs = x
            else:
                xp = x

    ck = jnp.stack(new_caches[0], axis=1).reshape(bc, depth, tc, GQA_KV_HEADS, HEAD_W)
    cv = jnp.stack(new_caches[1], axis=1).reshape(bc, depth, tc, GQA_KV_HEADS, HEAD_W)
    cdk = jnp.stack(new_caches[2], axis=1).reshape(bc, depth, tc, DIFF_HEADS, 2, DIFF_HEAD_DIM)
    cdv = jnp.stack(new_caches[3], axis=1).reshape(bc, depth, tc, DIFF_HEADS, 2 * DIFF_HEAD_DIM)
    cs = jnp.stack(new_caches[4], axis=1)
    return (xp, xs, ck, cv, cdk, cdv, cs)
```

```python
import functools
import math

import jax
import jax.numpy as jnp
from jax import lax
from jax.experimental import pallas as pl
from jax.experimental.pallas import tpu as pltpu

F32 = jnp.float32
BF16 = jnp.bfloat16

GRID_W = 64
BRANCH_W = 256
HEAD_W = 64
GQA_KV_HEADS = 2
RET_HEADS = 4
DIFF_HEADS = 4
DIFF_HEAD_DIM = 32
ROPE_THETA = 10000.0
EPS = 1e-6
LOG2_E = math.log2(math.e)
N_BRANCH = 4
LANES = 128
Q_TILE = 256
KEY_BLOCK = 256
VMEM_LIMIT_BYTES = 56 * 1024 * 1024

COL_A = 0
COL_B = 1024
COL_C = 1792
COL_D = 2816


def _vmem_bytes(shape, dtype):
    itemsize = jnp.dtype(dtype).itemsize
    sublanes = 8 * (4 // itemsize)
    rows = -(-shape[-2] // sublanes) * sublanes
    cols = -(-shape[-1] // LANES) * LANES
    return math.prod(shape[:-2]) * rows * cols * itemsize


def _silu(x):
    return x * jax.nn.sigmoid(x)


def _dot(a, b):
    return jnp.dot(a, b, preferred_element_type=F32)


def _group_ones(width):
    r = lax.broadcasted_iota(jnp.int32, (width, width), 0) >> 6
    c = lax.broadcasted_iota(jnp.int32, (width, width), 1) >> 6
    return jnp.where(r == c, 1.0, 0.0).astype(BF16)


def _head_rms(z, ones):
    zz = z * z
    hi = zz.astype(BF16)
    lo = (zz - hi.astype(F32)).astype(BF16)
    ss = _dot(hi, ones) + _dot(lo, ones)
    return z * lax.rsqrt(ss * (1.0 / HEAD_W) + EPS)


def _rope(x, cos_t, sin_t, half):
    lane = lax.broadcasted_iota(jnp.int32, (x.shape[0], LANES), 1)
    first = (lane & (2 * half - 1)) < half
    outs = []
    for j in range(x.shape[1] // LANES):
        xc = x[:, j * LANES:(j + 1) * LANES]
        partner = jnp.where(first, pltpu.roll(xc, LANES - half, 1), pltpu.roll(xc, half, 1))
        outs.append(xc * cos_t + partner * sin_t)
    return outs[0] if len(outs) == 1 else jnp.concatenate(outs, axis=1)


def _modulated_norm(x, mod_ref, ng_ref):
    ms = jnp.mean(x * x, axis=-1, keepdims=True)
    y = x * lax.rsqrt(ms + EPS) * ng_ref[...]
    return y * (1.0 + mod_ref[0, 1:2, :]) + mod_ref[0, 0:1, :]


def _adaln_kernel(cond_ref, w_ref, b_ref, o_ref):
    a = _silu(cond_ref[...]).astype(BF16)
    o_ref[0] = _dot(a, w_ref[0]) + b_ref[0]


def _adaln(cond, w_ada, b_ada):
    depth, d, d3 = w_ada.shape
    rows = cond.shape[0]
    return pl.pallas_call(
        _adaln_kernel,
        out_shape=jax.ShapeDtypeStruct((depth, rows, d3), F32),
        grid=(depth,),
        in_specs=[pl.BlockSpec((rows, d), lambda l: (0, 0)),
                  pl.BlockSpec((1, d, d3), lambda l: (l, 0, 0)),
                  pl.BlockSpec((1, 1, d3), lambda l: (l, 0, 0))],
        out_specs=pl.BlockSpec((1, rows, d3), lambda l: (l, 0, 0)),
        compiler_params=pltpu.CompilerParams(vmem_limit_bytes=VMEM_LIMIT_BYTES),
        name="adaln",
    )(cond, w_ada, b_ada.reshape(depth, 1, d3))


def _inproj_kernel(*refs, latent):
    if latent:
        (x_ref, mod_ref, ng_ref, w_ref, wdq_ref, qg_ref, kg_ref, c64_ref, s64_ref, c32_ref, s32_ref,
         oa_ref, og_ref, qb_ref, kb_ref, vb_ref, qc_ref, kc_ref, vc_ref, qd_ref, kd_ref, vd_ref) = refs
    else:
        (x_ref, mod_ref, ng_ref, w_ref, wdq_ref, qg_ref, kg_ref,
         oa_ref, og_ref, qb_ref, kb_ref, vb_ref, qc_ref, kc_ref, vc_ref, qd_ref, kd_ref, vd_ref,
         ck_ref, cv_ref, cdk_ref, cdv_ref) = refs
    for c in range(x_ref.shape[1] // Q_TILE):
        rows = slice(c * Q_TILE, (c + 1) * Q_TILE)
        hb = _modulated_norm(x_ref[0, rows, :], mod_ref, ng_ref).astype(BF16)

        z = _dot(hb, w_ref[:, COL_A:COL_A + 1024])
        oa_ref[0, rows, 0:256] = (z[:, 0:256] * _silu(z[:, 768:1024])).astype(BF16)
        oa_ref[0, rows, 256:512] = (z[:, 256:512] * z[:, 512:768]).astype(BF16)

        z = _dot(hb, w_ref[:, COL_B:COL_B + 768])
        q = _head_rms(z[:, 0:256], _group_ones(256)) * qg_ref[...]
        k = _head_rms(z[:, 256:384], _group_ones(128)) * kg_ref[...]
        v = z[:, 384:512]
        if latent:
            q = _rope(q, c64_ref[rows, :], s64_ref[rows, :], 32)
            k_att = _rope(k, c64_ref[rows, :], s64_ref[rows, :], 32)
        else:
            k_att = k
            ck_ref[0, rows, :] = k
            cv_ref[0, rows, :] = v
        qb_ref[0, c] = (q * (HEAD_W ** -0.5 * LOG2_E)).T.astype(BF16)
        vb_ref[0, :, rows] = v.T.astype(BF16)
        for h in range(GQA_KV_HEADS):
            kb_ref[0, h, rows, :] = k_att[:, h * HEAD_W:(h + 1) * HEAD_W].astype(BF16)
        og_ref[0, rows, 0:256] = _silu(z[:, 512:768]).astype(BF16)

        z = _dot(hb, w_ref[:, COL_C:COL_C + 1024])
        qc_ref[0, rows, :] = z[:, 0:256].astype(BF16)
        kc_ref[0, c] = (z[:, 256:512] * (HEAD_W ** -0.5)).T.astype(BF16)
        vc_ref[0, rows, :] = z[:, 512:768].astype(BF16)
        og_ref[0, rows, 256:512] = _silu(z[:, 768:1024]).astype(BF16)

        q = _dot(hb, wdq_ref[...])
        z = _dot(hb, w_ref[:, COL_D + 256:COL_D + 1024])
        k = z[:, 0:256]
        v = z[:, 256:512]
        if latent:
            q = _rope(q, c32_ref[rows, :], s32_ref[rows, :], 16)
            k_att = _rope(k, c32_ref[rows, :], s32_ref[rows, :], 16)
        else:
            k_att = k
            cdk_ref[0, rows, :] = k
            cdv_ref[0, rows, :] = v
        qd_ref[0, c] = (q * (DIFF_HEAD_DIM ** -0.5 * LOG2_E)).T.astype(BF16)
        vd_ref[0, :, rows] = v.T.astype(BF16)
        for h in range(DIFF_HEADS):
            kd_ref[0, h, rows, :] = k_att[:, h * HEAD_W:(h + 1) * HEAD_W].astype(BF16)
        og_ref[0, rows, 512:768] = _silu(z[:, 512:768]).astype(BF16)


def _inproj(x, mod, ng, w, w_dq, qg, kg, rope_tabs, *, latent, tm):
    b, t, d = x.shape
    nt = t // tm
    per_batch_mod = mod.shape[0] > 1
    tok = lambda w: pl.BlockSpec((1, tm, w), lambda i, j: (i, j, 0))
    tok_t = lambda w: pl.BlockSpec((1, w, tm), lambda i, j: (i, 0, j))
    heads = lambda n: pl.BlockSpec((1, n, tm, HEAD_W), lambda i, j: (i, 0, j, 0))
    const = lambda shape: pl.BlockSpec(shape, lambda i, j: (0,) * len(shape), pipeline_mode=pl.Buffered(1))
    in_specs = [tok(d),
                pl.BlockSpec((1, 3, d), (lambda i, j: (i, 0, 0)) if per_batch_mod else (lambda i, j: (0, 0, 0))),
                const((1, d)), const(w.shape), const(w_dq.shape), const((1, 256)), const((1, 128))]
    args = [x, mod, ng, w, w_dq, qg, kg]
    if latent:
        in_specs += [pl.BlockSpec((tm, LANES), lambda i, j: (j, 0))] * 4
        args += list(rope_tabs)
    sd = jax.ShapeDtypeStruct
    qn = tm // Q_TILE
    q_tiles = lambda w: pl.BlockSpec((1, qn, w, Q_TILE), lambda i, j: (i, j, 0, 0))
    out_shape = [sd((b, t, 512), BF16), sd((b, t, 768), BF16),
                 sd((b, t // Q_TILE, 256, Q_TILE), BF16), sd((b, GQA_KV_HEADS, t, HEAD_W), BF16), sd((b, 128, t), BF16),
                 sd((b, t, 256), BF16), sd((b, t // Q_TILE, 256, Q_TILE), BF16), sd((b, t, 256), BF16),
                 sd((b, t // Q_TILE, 512, Q_TILE), BF16), sd((b, DIFF_HEADS, t, HEAD_W), BF16), sd((b, 256, t), BF16)]
    out_specs = [tok(512), tok(768), q_tiles(256), heads(GQA_KV_HEADS), tok_t(128),
                 tok(256), q_tiles(256), tok(256), q_tiles(512), heads(DIFF_HEADS), tok_t(256)]
    if not latent:
        out_shape += [sd((b, t, 128), F32), sd((b, t, 128), F32), sd((b, t, 256), F32), sd((b, t, 256), F32)]
        out_specs += [tok(128), tok(128), tok(256), tok(256)]
    return pl.pallas_call(
        functools.partial(_inproj_kernel, latent=latent),
        out_shape=out_shape,
        grid=(b, nt),
        in_specs=in_specs,
        out_specs=out_specs,
        compiler_params=pltpu.CompilerParams(
            dimension_semantics=("parallel", "parallel"), vmem_limit_bytes=VMEM_LIMIT_BYTES),
        name="inproj_lat" if latent else "inproj_ctx",
    )(*args)


def _attn_kernel(*refs, n_kv, n_q, kb, has_cache, diff, lambda_init):
    if has_cache:
        qt_ref, kc_ref, vc_ref, kn_ref, vn_ref, lam_ref, gain_ref, o_ref, s_even, s_odd = refs
        sources = [(kc_ref, vc_ref), (kn_ref, vn_ref)]
    else:
        qt_ref, kn_ref, vn_ref, lam_ref, gain_ref, o_ref, s_even, s_odd = refs
        sources = [(kn_ref, vn_ref)]
    blocks = [(k_ref, v_ref, a) for k_ref, v_ref in sources for a in range(0, k_ref.shape[2], kb)]
    lanes = 2 * Q_TILE
    ones = jnp.ones((16, kb), BF16)
    if diff:
        lp = lam_ref[...]
        lam = (jnp.exp(jnp.sum(lp[0:1] * lp[1:2], axis=-1, keepdims=True))
               - jnp.exp(jnp.sum(lp[2:3] * lp[3:4], axis=-1, keepdims=True)) + lambda_init)

    def stage(i_new, s_new, i_old, s_old, m_old):
        if i_new is not None:
            h_new, u_new = i_new // n_q, i_new % n_q
            q2 = qt_ref[0, u_new, pl.ds(pl.multiple_of(h_new * LANES, LANES), LANES), :]
            qt = jnp.concatenate([q2[:HEAD_W], q2[HEAD_W:]], axis=1)
        if i_old is not None:
            h_old, u_old = i_old // n_q, i_old % n_q
            v_rows = pl.ds(pl.multiple_of(h_old * HEAD_W, HEAD_W), HEAD_W)
            m8_old = jnp.broadcast_to(m_old, (8, lanes))
        m8, acc = None, None
        for j, (k_ref, v_ref, a) in enumerate(blocks):
            if i_new is not None:
                s = _dot(k_ref[0, h_new, a:a + kb, :], qt)
                s_new[j * kb:(j + 1) * kb, :] = s
                mj = jnp.max(s.reshape(kb // 8, 8, lanes), axis=0)
                m8 = mj if m8 is None else jnp.maximum(m8, mj)
            if i_old is not None:
                s = s_old[j * kb:(j + 1) * kb, :]
                p = jnp.exp2(s.reshape(kb // 8, 8, lanes) - m8_old[None]).reshape(kb, lanes).astype(BF16)
                vt = jnp.concatenate([v_ref[0, v_rows, a:a + kb], ones], axis=0)
                part = _dot(vt, p)
                acc = part if acc is None else acc + part
        if i_old is not None:
            o = acc[:HEAD_W] / acc[HEAD_W:HEAD_W + 1]
            if diff:
                od = o[:, :Q_TILE] - lam * o[:, Q_TILE:]
                ms = jnp.mean(od * od, axis=0, keepdims=True)
                y = od * lax.rsqrt(ms + EPS) * gain_ref[...] * (1.0 - lambda_init)
                o_ref[0, u_old, pl.ds(pl.multiple_of(h_old * HEAD_W, HEAD_W), HEAD_W), :] = y.astype(o_ref.dtype)
            else:
                y = jnp.concatenate([o[:, :Q_TILE], o[:, Q_TILE:]], axis=0)
                o_ref[0, u_old, pl.ds(pl.multiple_of(h_old * LANES, LANES), LANES), :] = y.astype(o_ref.dtype)
        return None if i_new is None else jnp.max(m8, axis=0, keepdims=True)

    n_units = n_kv * n_q
    assert n_units % 2 == 0

    def two_stages(t, m_even):
        m_odd = stage(2 * t + 1, s_odd, 2 * t, s_even, m_even)
        return stage(2 * t + 2, s_even, 2 * t + 1, s_odd, m_odd)

    m_even = lax.fori_loop(0, n_units // 2 - 1, two_stages, stage(0, s_even, None, None, None))
    m_odd = stage(n_units - 1, s_odd, n_units - 2, s_even, m_even)
    stage(None, None, n_units - 1, s_odd, m_odd)


def _attention(qt, k_parts, vt_parts, lam_p, gain, *, diff, lambda_init):
    b, n_q, qrows, _ = qt.shape
    n_kv = k_parts[0].shape[1]
    lens = [k.shape[2] for k in k_parts]
    assert qrows == n_kv * LANES and all(v.shape[1] == n_kv * HEAD_W for v in vt_parts)
    kb = KEY_BLOCK if all(n % KEY_BLOCK == 0 for n in lens) else math.gcd(*lens)
    kern = functools.partial(_attn_kernel, n_kv=n_kv, n_q=n_q, kb=kb, has_cache=len(k_parts) == 2, diff=diff,
                             lambda_init=lambda_init)
    kv = [a for pair in zip(k_parts, vt_parts) for a in pair]
    score_bytes = 2 * sum(lens) * 2 * Q_TILE * 4
    out_bytes = 2 * _vmem_bytes((n_q, 256, Q_TILE), BF16)
    operand_bytes = sum(_vmem_bytes(a.shape[1:], a.dtype) for a in [qt] + kv)
    double_buffered = score_bytes + out_bytes + 2 * operand_bytes <= VMEM_LIMIT_BYTES
    mode = {} if double_buffered else dict(pipeline_mode=pl.Buffered(1))
    whole = lambda a: pl.BlockSpec((1,) + a.shape[1:], lambda i: (i,) + (0,) * (a.ndim - 1), **mode)
    return pl.pallas_call(
        kern,
        out_shape=jax.ShapeDtypeStruct((b, n_q, 256, Q_TILE), BF16),
        grid=(b,),
        in_specs=[whole(qt)] + [whole(a) for a in kv] + [pl.BlockSpec(lam_p.shape, lambda i: (0, 0)),
                                                          pl.BlockSpec(gain.shape, lambda i: (0, 0))],
        out_specs=pl.BlockSpec((1, n_q, 256, Q_TILE), lambda i: (i, 0, 0, 0)),
        scratch_shapes=[pltpu.VMEM((sum(lens), 2 * Q_TILE), F32), pltpu.VMEM((sum(lens), 2 * Q_TILE), F32)],
        compiler_params=pltpu.CompilerParams(
            dimension_semantics=("parallel",), vmem_limit_bytes=VMEM_LIMIT_BYTES),
        name="diff_attn" if diff else "gqa_attn",
    )(qt, *kv, lam_p, gain)


def _retention_kernel(q_ref, kt_ref, v_ref, s0_ref, dec_ref, o_ref, sn_ref, kv_scr, sf_scr, sb_scr, *, n_chunks):
    c_len = Q_TILE
    dec = dec_ref[...]
    lg = jnp.minimum(dec, 0.0) - jnp.log1p(jnp.exp(-jnp.abs(dec)))
    row = lax.broadcasted_iota(jnp.int32, (c_len, c_len), 0)
    col = lax.broadcasted_iota(jnp.int32, (c_len, c_len), 1)
    rel = (row - col).astype(F32)
    pos = lax.broadcasted_iota(jnp.int32, (c_len, HEAD_W), 0).astype(F32)
    pos_t = lax.broadcasted_iota(jnp.int32, (HEAD_W, c_len), 1).astype(F32)
    masks, qdf, qdb, kdf, kdb, cdf, cdb = [], [], [], [], [], [], []
    for h in range(RET_HEADS):
        lf = lg[0:1, h:h + 1]
        lb = lg[1:2, h:h + 1]
        masks.append(jnp.where(rel >= 0, jnp.exp(lf * jnp.maximum(rel, 0.0)), 0.0)
                     + jnp.where(rel <= 0, jnp.exp(lb * jnp.maximum(-rel, 0.0)), 0.0))
        qdf.append(jnp.exp(lf * (pos + 1.0)))
        qdb.append(jnp.exp(lb * (c_len - pos)))
        kdf.append(jnp.exp(lf * (c_len - 1.0 - pos_t)))
        kdb.append(jnp.exp(lb * pos_t))
        cdf.append(jnp.exp(lf * c_len))
        cdb.append(jnp.exp(lb * c_len))

    def chunk_of(ref, c, h):
        start = pl.multiple_of(c * c_len, c_len)
        return ref[0, pl.ds(start, c_len), h * HEAD_W:(h + 1) * HEAD_W]

    unroll = 2 if n_chunks % 2 == 0 else 1

    def kv_body(c, carry):
        for h in range(RET_HEADS):
            kt = kt_ref[0, c, h * HEAD_W:(h + 1) * HEAD_W, :].astype(F32)
            vc = chunk_of(v_ref, c, h)
            kv_scr[c, h, 0] = _dot((kt * kdf[h]).astype(BF16), vc)
            kv_scr[c, h, 1] = _dot((kt * kdb[h]).astype(BF16), vc)
        return carry

    lax.fori_loop(0, n_chunks, kv_body, 0, unroll=unroll)

    def scan_body(i, states):
        sf, sb = states
        cb = n_chunks - 1 - i
        new_f, new_b = [], []
        for h in range(RET_HEADS):
            sf_scr[i, h] = sf[h]
            sb_scr[cb, h] = sb[h]
            new_f.append(sf[h] * cdf[h] + kv_scr[i, h, 0])
            new_b.append(sb[h] * cdb[h] + kv_scr[cb, h, 1])
        return tuple(new_f), tuple(new_b)

    sf_fin, sb_fin = lax.fori_loop(0, n_chunks, scan_body,
                                   (tuple(s0_ref[0, 0, h] for h in range(RET_HEADS)),
                                    tuple(s0_ref[0, 1, h] for h in range(RET_HEADS))))
    for h in range(RET_HEADS):
        sn_ref[0, 0, h] = sf_fin[h]
        sn_ref[0, 1, h] = sb_fin[h]

    def out_body(c, carry):
        outs = []
        for h in range(RET_HEADS):
            qc, vc = chunk_of(q_ref, c, h), chunk_of(v_ref, c, h)
            a = _dot(qc, kt_ref[0, c, h * HEAD_W:(h + 1) * HEAD_W, :]) * masks[h]
            o = _dot(a.astype(BF16), vc)
            o = o + _dot(qc, sf_scr[c, h].astype(BF16)) * qdf[h]
            o = o + _dot(qc, sb_scr[c, h].astype(BF16)) * qdb[h]
            ms = jnp.mean(o * o, axis=-1, keepdims=True)
            outs.append(o * lax.rsqrt(ms + EPS))
        start = pl.multiple_of(c * c_len, c_len)
        o_ref[0, pl.ds(start, c_len), :] = jnp.concatenate(outs, axis=-1).astype(o_ref.dtype)
        return carry

    lax.fori_loop(0, n_chunks, out_body, 0, unroll=unroll)


def _retention(q, kt, v, s0, decay):
    b, t, w = q.shape
    n_chunks = kt.shape[1]
    assert n_chunks * Q_TILE == t
    tok = pl.BlockSpec((1, t, w), lambda i: (i, 0, 0))
    st = pl.BlockSpec((1, 2, RET_HEADS, HEAD_W, HEAD_W), lambda i: (i, 0, 0, 0, 0))
    state_scr = pltpu.VMEM((n_chunks, RET_HEADS, HEAD_W, HEAD_W), F32)
    return pl.pallas_call(
        functools.partial(_retention_kernel, n_chunks=n_chunks),
        out_shape=[jax.ShapeDtypeStruct((b, t, w), BF16),
                   jax.ShapeDtypeStruct((b, 2, RET_HEADS, HEAD_W, HEAD_W), F32)],
        grid=(b,),
        in_specs=[tok, pl.BlockSpec((1, n_chunks, w, Q_TILE), lambda i: (i, 0, 0, 0)), tok, st,
                  pl.BlockSpec((2, RET_HEADS), lambda i: (0, 0))],
        out_specs=[tok, st],
        scratch_shapes=[pltpu.VMEM((n_chunks, RET_HEADS, 2, HEAD_W, HEAD_W), F32), state_scr, state_scr],
        compiler_params=pltpu.CompilerParams(
            dimension_semantics=("parallel",), vmem_limit_bytes=VMEM_LIMIT_BYTES),
        name="retention",
    )(q, kt, v, s0, decay)


def _merge_kernel(x_ref, mod_ref, ng_ref, oa_ref, prev_ref, next_ref, og_ref, yb_ref, yc_ref, yd_ref,
                  cw_ref, wg_ref, wbr_ref, wo_ref, fg_ref, o_ref, *, tm, last):
    j = pl.program_id(1)

    g = oa_ref[0, :, 256:512].astype(F32)
    row = lax.broadcasted_iota(jnp.int32, g.shape, 0)
    g_first = jnp.where(j == 0, 0.0, prev_ref[0, 7:8, 256:512].astype(F32))
    g_last = jnp.where(j == pl.num_programs(1) - 1, 0.0, next_ref[0, 0:1, 256:512].astype(F32))
    g_prev = jnp.where(row == 0, g_first, pltpu.roll(g, 1, 0))
    g_next = jnp.where(row == tm - 1, g_last, pltpu.roll(g, tm - 1, 0))
    conv = g_prev * cw_ref[0:1, :] + g * cw_ref[1:2, :] + g_next * cw_ref[2:3, :]

    for c in range(tm // Q_TILE):
        rows = slice(c * Q_TILE, (c + 1) * Q_TILE)
        x = x_ref[0, rows, :]
        hb = _modulated_norm(x, mod_ref, ng_ref).astype(BF16)
        token_major = lambda ref: ref[0, c].astype(F32).T.astype(BF16)
        ys = [(oa_ref[0, rows, 0:256].astype(F32) * conv[rows]).astype(BF16),
              token_major(yb_ref) * og_ref[0, rows, 0:256],
              yc_ref[0, rows, :] * og_ref[0, rows, 256:512],
              token_major(yd_ref) * og_ref[0, rows, 512:768]]
        merged = None
        for i in range(N_BRANCH):
            gate = jax.nn.sigmoid(_dot(hb, wg_ref[:, i * 1024:(i + 1) * 1024]))
            term = gate * _dot(ys[i], wbr_ref[i])
            merged = term if merged is None else merged + term
        out = _dot(merged.astype(BF16), wo_ref[...])
        xn = x + mod_ref[0, 2:3, :] * out
        if last:
            ms = jnp.mean(xn * xn, axis=-1, keepdims=True)
            xn = xn * lax.rsqrt(ms + EPS) * fg_ref[...]
        o_ref[0, rows, :] = xn


def _merge(x, mod, ng, oa, og, yb, yc, yd, conv_w, wg, wbr, wo, fg, *, tm, last):
    b, t, d = x.shape
    nt = t // tm
    rb = tm // 8
    per_batch_mod = mod.shape[0] > 1
    tok = lambda w: pl.BlockSpec((1, tm, w), lambda i, j: (i, j, 0))
    q_tiles = pl.BlockSpec((1, tm // Q_TILE, 256, Q_TILE), lambda i, j: (i, j, 0, 0))
    const = lambda shape: pl.BlockSpec(shape, lambda i, j: (0,) * len(shape), pipeline_mode=pl.Buffered(1))
    in_specs = [tok(d),
                pl.BlockSpec((1, 3, d), (lambda i, j: (i, 0, 0)) if per_batch_mod else (lambda i, j: (0, 0, 0))),
                const((1, d)), tok(512),
                pl.BlockSpec((1, 8, 512), lambda i, j: (i, jnp.maximum(j * rb - 1, 0), 0)),
                pl.BlockSpec((1, 8, 512), lambda i, j: (i, jnp.minimum((j + 1) * rb, t // 8 - 1), 0)),
                tok(768), q_tiles, tok(256), q_tiles,
                const(conv_w.shape), const(wg.shape), const(wbr.shape), const(wo.shape), const((1, d))]
    return pl.pallas_call(
        functools.partial(_merge_kernel, tm=tm, last=last),
        out_shape=jax.ShapeDtypeStruct((b, t, d), F32),
        grid=(b, nt),
        in_specs=in_specs,
        out_specs=tok(d),
        compiler_params=pltpu.CompilerParams(
            dimension_semantics=("parallel", "parallel"), vmem_limit_bytes=VMEM_LIMIT_BYTES),
        name="merge",
    )(x, mod, ng, oa, oa, oa, og, yb, yc, yd, conv_w, wg, wbr, wo, fg)


def _widen_diff_q(w):
    d = w.shape[0]
    dq = w[:, COL_D:COL_D + 256].reshape(d, DIFF_HEADS, 2, 1, DIFF_HEAD_DIM)
    eye = jnp.eye(2, dtype=w.dtype).reshape(1, 1, 2, 2, 1)
    return (dq * eye).reshape(d, 2 * 256).astype(BF16)


def _rope_tables(n_tokens, head_dim):
    rows = n_tokens // GRID_W
    row = jnp.repeat(jnp.arange(rows, dtype=F32), GRID_W)
    col = jnp.tile(jnp.arange(GRID_W, dtype=F32), rows)
    n_axis = head_dim // 4
    inv_freq = ROPE_THETA ** (-jnp.arange(n_axis, dtype=F32) / n_axis)
    ang = jnp.concatenate([row[:, None] * inv_freq, col[:, None] * inv_freq], axis=-1)
    cos, sin = jnp.cos(ang), jnp.sin(ang)
    reps = LANES // head_dim
    return (jnp.tile(jnp.concatenate([cos, cos], axis=-1), (1, reps)),
            jnp.tile(jnp.concatenate([-sin, sin], axis=-1), (1, reps)))


def _pick(n, pref):
    return pref if n % pref == 0 else n


def kernel(x_prompt, x_sample, cache_gqa_k, cache_gqa_v, cache_diff_k, cache_diff_v, state_ret, c, c_ctx,
           w_ada, b_ada, norm_gain, w_in, conv_w, gqa_q_gain, gqa_k_gain, ret_decay, diff_lambda,
           diff_norm_gain, w_branch, w_mgate, w_out, final_gain):
    depth = w_in.shape[0]
    bc, tc, d = x_prompt.shape
    bl, tl, _ = x_sample.shape

    cond = jnp.concatenate([c_ctx[None], c], axis=0)
    n_cond = cond.shape[0]
    cond = jnp.pad(cond, ((0, -n_cond % 8), (0, 0)))
    mod = _adaln(cond, w_ada.astype(BF16), b_ada)
    mod = mod.reshape(depth, cond.shape[0], 3, d)

    rope = _rope_tables(tl, HEAD_W) + _rope_tables(tl, DIFF_HEAD_DIM)
    fg = final_gain.reshape(1, d)
    zero_state = jnp.zeros((bc, 2, RET_HEADS, HEAD_W, HEAD_W), F32)

    def to_heads(cache):
        b, p = cache.shape[:2]
        return jnp.transpose(cache.reshape(b, p, -1, HEAD_W), (0, 2, 1, 3)).astype(BF16)

    def to_features(cache):
        b, p = cache.shape[:2]
        return jnp.transpose(cache.reshape(b, p, -1), (0, 2, 1)).astype(BF16)

    xp, xs = x_prompt, x_sample
    new_caches = ([], [], [], [], [])
    for l in range(depth):
        lambda_init = 0.8 - 0.6 * math.exp(-0.3 * l)
        w_l, w_dq = w_in[l].astype(BF16), _widen_diff_q(w_in[l])
        ng = norm_gain[l].reshape(1, d)
        qg = jnp.tile(gqa_q_gain[l], 256 // HEAD_W).reshape(1, 256)
        kg = jnp.tile(gqa_k_gain[l], 128 // HEAD_W).reshape(1, 128)
        dgain = diff_norm_gain[l].reshape(HEAD_W, 1)
        wg, wbr, wo = w_mgate[l].astype(BF16), w_branch[l].astype(BF16), w_out[l].astype(BF16)

        for latent in (False, True):
            x = xs if latent else xp
            b, t, _ = x.shape
            m = mod[l, 1:1 + bl] if latent else mod[l, 0:1]
            tm = _pick(t, 1024) if latent else _pick(t, 256)
            outs = _inproj(x, m, ng, w_l, w_dq, qg, kg, rope, latent=latent, tm=tm)
            oa, og, qb, kb, vb, qc, kc, vc, qd, kd, vd = outs[:11]
            kb, vb, kd, vd = [kb], [vb], [kd], [vd]
            if latent:
                kb.insert(0, to_heads(cache_gqa_k[:, l]))
                vb.insert(0, to_features(cache_gqa_v[:, l]))
                kd.insert(0, to_heads(cache_diff_k[:, l]))
                vd.insert(0, to_features(cache_diff_v[:, l]))
                s0 = state_ret[:, l]
            else:
                for lst, a in zip(new_caches[:4], outs[11:]):
                    lst.append(a)
                s0 = zero_state
            yb = _attention(qb, kb, vb, diff_lambda[l], dgain, diff=False, lambda_init=lambda_init)
            yd = _attention(qd, kd, vd, diff_lambda[l], dgain, diff=True, lambda_init=lambda_init)
            yc, s_new = _retention(qc, kc, vc, s0, ret_decay[l])
            if not latent:
                new_caches[4].append(s_new)
            x = _merge(x, m, ng, oa, og, yb, yc, yd, conv_w[l], wg, wbr, wo, fg, tm=tm, last=(l == depth - 1))
            if latent:
                xs = x
            else:
                xp = x

    ck = jnp.stack(new_caches[0], axis=1).reshape(bc, depth, tc, GQA_KV_HEADS, HEAD_W)
    cv = jnp.stack(new_caches[1], axis=1).reshape(bc, depth, tc, GQA_KV_HEADS, HEAD_W)
    cdk = jnp.stack(new_caches[2], axis=1).reshape(bc, depth, tc, DIFF_HEADS, 2, DIFF_HEAD_DIM)
    cdv = jnp.stack(new_caches[3], axis=1).reshape(bc, depth, tc, DIFF_HEADS, 2 * DIFF_HEAD_DIM)
    cs = jnp.stack(new_caches[4], axis=1)
    return (xp, xs, ck, cv, cdk, cdv, cs)
```

```python
import functools
import math

import jax
import jax.numpy as jnp
import numpy as np
from jax import lax
from jax.experimental import pallas as pl
from jax.experimental.pallas import tpu as pltpu

F32 = jnp.float32
BF16 = jnp.bfloat16

GRID_W = 64
BRANCH_W = 256
HEAD_W = 64
GQA_KV_HEADS = 2
RET_HEADS = 4
DIFF_HEADS = 4
DIFF_HEAD_DIM = 32
ROPE_THETA = 10000.0
EPS = 1e-6
LOG2_E = math.log2(math.e)
N_BRANCH = 4
LANES = 128
Q_TILE = 256
KEY_BLOCK = 256
TOKEN_TILE = 1024
VMEM_LIMIT_BYTES = 56 * 1024 * 1024

COL_A = 0
COL_B = 1024
COL_C = 1792
COL_D = 2816


def _vmem_bytes(shape, dtype):
    itemsize = jnp.dtype(dtype).itemsize
    sublanes = 8 * (4 // itemsize)
    rows = -(-shape[-2] // sublanes) * sublanes
    cols = -(-shape[-1] // LANES) * LANES
    return math.prod(shape[:-2]) * rows * cols * itemsize


def _silu(x):
    return x * jax.nn.sigmoid(x)


def _dot(a, b):
    return jnp.dot(a, b, preferred_element_type=F32)


def _group_ones(width):
    r = lax.broadcasted_iota(jnp.int32, (width, width), 0) >> 6
    c = lax.broadcasted_iota(jnp.int32, (width, width), 1) >> 6
    return jnp.where(r == c, 1.0, 0.0).astype(BF16)


def _head_rms(z, ones):
    zz = z * z
    hi = zz.astype(BF16)
    lo = (zz - hi.astype(F32)).astype(BF16)
    ss = _dot(hi, ones) + _dot(lo, ones)
    return z * lax.rsqrt(ss * (1.0 / HEAD_W) + EPS)


def _rope(x, cos_t, sin_t, half):
    lane = lax.broadcasted_iota(jnp.int32, (x.shape[0], LANES), 1)
    first = (lane & (2 * half - 1)) < half
    outs = []
    for j in range(x.shape[1] // LANES):
        xc = x[:, j * LANES:(j + 1) * LANES]
        partner = jnp.where(first, pltpu.roll(xc, LANES - half, 1), pltpu.roll(xc, half, 1))
        outs.append(xc * cos_t + partner * sin_t)
    return outs[0] if len(outs) == 1 else jnp.concatenate(outs, axis=1)


def _modulated_norm(x, mod_ref, ng_ref):
    ms = jnp.mean(x * x, axis=-1, keepdims=True)
    y = x * lax.rsqrt(ms + EPS) * ng_ref[...]
    return y * (1.0 + mod_ref[0, 1:2, :]) + mod_ref[0, 0:1, :]


def _adaln_kernel(cond_ref, w_ref, b_ref, o_ref):
    a = _silu(cond_ref[...]).astype(BF16)
    o_ref[0] = _dot(a, w_ref[0]) + b_ref[0]


def _adaln(cond, w_ada, b_ada):
    depth, d, d3 = w_ada.shape
    rows = cond.shape[0]
    return pl.pallas_call(
        _adaln_kernel,
        out_shape=jax.ShapeDtypeStruct((depth, rows, d3), F32),
        grid=(depth,),
        in_specs=[pl.BlockSpec((rows, d), lambda l: (0, 0)),
                  pl.BlockSpec((1, d, d3), lambda l: (l, 0, 0)),
                  pl.BlockSpec((1, 1, d3), lambda l: (l, 0, 0))],
        out_specs=pl.BlockSpec((1, rows, d3), lambda l: (l, 0, 0)),
        compiler_params=pltpu.CompilerParams(vmem_limit_bytes=VMEM_LIMIT_BYTES),
        name="adaln",
    )(cond, w_ada, b_ada.reshape(depth, 1, d3))


def _inproj_kernel(*refs, latent):
    if latent:
        (x_ref, mod_ref, ng_ref, w_ref, wdq_ref, qg_ref, kg_ref, c64_ref, s64_ref, c32_ref, s32_ref,
         oa_ref, og_ref, qb_ref, kb_ref, vb_ref, qc_ref, kc_ref, vc_ref, qd_ref, kd_ref, vd_ref) = refs
    else:
        (x_ref, mod_ref, ng_ref, w_ref, wdq_ref, qg_ref, kg_ref,
         oa_ref, og_ref, qb_ref, kb_ref, vb_ref, qc_ref, kc_ref, vc_ref, qd_ref, kd_ref, vd_ref,
         ck_ref, cv_ref, cdk_ref, cdv_ref) = refs
    nb, tm = x_ref.shape[:2]
    for bi, c in [(bi, c) for bi in range(nb) for c in range(tm // Q_TILE)]:
        rows = slice(c * Q_TILE, (c + 1) * Q_TILE)
        hb = _modulated_norm(x_ref[bi, rows, :], mod_ref, ng_ref).astype(BF16)

        z = _dot(hb, w_ref[:, COL_A:COL_A + 1024])
        oa_ref[bi, rows, 0:256] = (z[:, 0:256] * _silu(z[:, 768:1024])).astype(BF16)
        oa_ref[bi, rows, 256:512] = (z[:, 256:512] * z[:, 512:768]).astype(BF16)

        z = _dot(hb, w_ref[:, COL_B:COL_B + 768])
        q = _head_rms(z[:, 0:256], _group_ones(256)) * qg_ref[...]
        k = _head_rms(z[:, 256:384], _group_ones(128)) * kg_ref[...]
        v = z[:, 384:512]
        if latent:
            q = _rope(q, c64_ref[rows, :], s64_ref[rows, :], 32)
            k_att = _rope(k, c64_ref[rows, :], s64_ref[rows, :], 32)
        else:
            k_att = k
            ck_ref[bi, rows, :] = k
            cv_ref[bi, rows, :] = v
        qb_ref[bi, c] = (q * (HEAD_W ** -0.5 * LOG2_E)).T.astype(BF16)
        vb_ref[bi, :, rows] = v.T.astype(BF16)
        for h in range(GQA_KV_HEADS):
            kb_ref[bi, h, rows, :] = k_att[:, h * HEAD_W:(h + 1) * HEAD_W].astype(BF16)
        og_ref[bi, rows, 0:256] = _silu(z[:, 512:768]).astype(BF16)

        z = _dot(hb, w_ref[:, COL_C:COL_C + 1024])
        qc_ref[bi, rows, :] = z[:, 0:256].astype(BF16)
        kc_ref[bi, c] = (z[:, 256:512] * (HEAD_W ** -0.5)).T.astype(BF16)
        vc_ref[bi, rows, :] = z[:, 512:768].astype(BF16)
        og_ref[bi, rows, 256:512] = _silu(z[:, 768:1024]).astype(BF16)

        q = _dot(hb, wdq_ref[...])
        z = _dot(hb, w_ref[:, COL_D + 256:COL_D + 1024])
        k = z[:, 0:256]
        v = z[:, 256:512]
        if latent:
            q = _rope(q, c32_ref[rows, :], s32_ref[rows, :], 16)
            k_att = _rope(k, c32_ref[rows, :], s32_ref[rows, :], 16)
        else:
            k_att = k
            cdk_ref[bi, rows, :] = k
            cdv_ref[bi, rows, :] = v
        qd_ref[bi, c] = (q * (DIFF_HEAD_DIM ** -0.5 * LOG2_E)).T.astype(BF16)
        vd_ref[bi, :, rows] = v.T.astype(BF16)
        for h in range(DIFF_HEADS):
            kd_ref[bi, h, rows, :] = k_att[:, h * HEAD_W:(h + 1) * HEAD_W].astype(BF16)
        og_ref[bi, rows, 512:768] = _silu(z[:, 512:768]).astype(BF16)


def _layer_spec(a, layer):
    return pl.BlockSpec((None,) + a.shape[1:], lambda i, j: (layer,) + (0,) * (a.ndim - 1),
                        pipeline_mode=pl.Buffered(1))


def _inproj(x, mod, ng, w, w_dq, qg, kg, rope_tabs, *, layer, latent, nb, tm):
    b, t, d = x.shape
    nt = t // tm
    per_batch_mod = mod.shape[0] > 1
    assert b % nb == 0 and not (per_batch_mod and nb > 1)
    tok = lambda w: pl.BlockSpec((nb, tm, w), lambda i, j: (i, j, 0))
    tok_t = lambda w: pl.BlockSpec((nb, w, tm), lambda i, j: (i, 0, j))
    heads = lambda n: pl.BlockSpec((nb, n, tm, HEAD_W), lambda i, j: (i, 0, j, 0))
    const = lambda shape: pl.BlockSpec(shape, lambda i, j: (0,) * len(shape), pipeline_mode=pl.Buffered(1))
    in_specs = [tok(d),
                pl.BlockSpec((1, 3, d), (lambda i, j: (i, 0, 0)) if per_batch_mod else (lambda i, j: (0, 0, 0))),
                const((1, d)), _layer_spec(w, layer), _layer_spec(w_dq, layer), const((1, 256)), const((1, 128))]
    args = [x, mod, ng, w, w_dq, qg, kg]
    if latent:
        in_specs += [pl.BlockSpec((tm, LANES), lambda i, j: (j, 0))] * 4
        args += list(rope_tabs)
    sd = jax.ShapeDtypeStruct
    qn = tm // Q_TILE
    q_tiles = lambda w: pl.BlockSpec((nb, qn, w, Q_TILE), lambda i, j: (i, j, 0, 0))
    out_shape = [sd((b, t, 512), BF16), sd((b, t, 768), BF16),
                 sd((b, t // Q_TILE, 256, Q_TILE), BF16), sd((b, GQA_KV_HEADS, t, HEAD_W), BF16), sd((b, 128, t), BF16),
                 sd((b, t, 256), BF16), sd((b, t // Q_TILE, 256, Q_TILE), BF16), sd((b, t, 256), BF16),
                 sd((b, t // Q_TILE, 512, Q_TILE), BF16), sd((b, DIFF_HEADS, t, HEAD_W), BF16), sd((b, 256, t), BF16)]
    out_specs = [tok(512), tok(768), q_tiles(256), heads(GQA_KV_HEADS), tok_t(128),
                 tok(256), q_tiles(256), tok(256), q_tiles(512), heads(DIFF_HEADS), tok_t(256)]
    if not latent:
        out_shape += [sd((b, t, 128), F32), sd((b, t, 128), F32), sd((b, t, 256), F32), sd((b, t, 256), F32)]
        out_specs += [tok(128), tok(128), tok(256), tok(256)]
    return pl.pallas_call(
        functools.partial(_inproj_kernel, latent=latent),
        out_shape=out_shape,
        grid=(b // nb, nt),
        in_specs=in_specs,
        out_specs=out_specs,
        compiler_params=pltpu.CompilerParams(
            dimension_semantics=("parallel", "parallel"), vmem_limit_bytes=VMEM_LIMIT_BYTES),
        name="inproj_lat" if latent else "inproj_ctx",
    )(*args)


def _attn_kernel(*refs, n_kv, n_q, kb, has_cache, diff, lambda_init):
    if has_cache:
        qt_ref, kc_ref, vc_ref, kn_ref, vn_ref, lam_ref, gain_ref, o_ref, s_even, s_odd = refs
        sources = [(kc_ref, vc_ref), (kn_ref, vn_ref)]
    else:
        qt_ref, kn_ref, vn_ref, lam_ref, gain_ref, o_ref, s_even, s_odd = refs
        sources = [(kn_ref, vn_ref)]
    blocks = [(k_ref, v_ref, a) for k_ref, v_ref in sources for a in range(0, k_ref.shape[2], kb)]
    lanes = 2 * Q_TILE
    ones = jnp.ones((16, kb), BF16)
    if diff:
        lp = lam_ref[...]
        lam = (jnp.exp(jnp.sum(lp[0:1] * lp[1:2], axis=-1, keepdims=True))
               - jnp.exp(jnp.sum(lp[2:3] * lp[3:4], axis=-1, keepdims=True)) + lambda_init)

    def stage(i_new, s_new, i_old, s_old, m_old):
        if i_new is not None:
            h_new, u_new = i_new // n_q, i_new % n_q
            q2 = qt_ref[0, u_new, pl.ds(pl.multiple_of(h_new * LANES, LANES), LANES), :]
            qt = jnp.concatenate([q2[:HEAD_W], q2[HEAD_W:]], axis=1)
        if i_old is not None:
            h_old, u_old = i_old // n_q, i_old % n_q
            v_rows = pl.ds(pl.multiple_of(h_old * HEAD_W, HEAD_W), HEAD_W)
            m8_old = jnp.broadcast_to(m_old, (8, lanes))
        m8, acc = None, None
        for j, (k_ref, v_ref, a) in enumerate(blocks):
            if i_new is not None:
                s = _dot(k_ref[0, h_new, a:a + kb, :], qt)
                s_new[j * kb:(j + 1) * kb, :] = s
                mj = jnp.max(s.reshape(kb // 8, 8, lanes), axis=0)
                m8 = mj if m8 is None else jnp.maximum(m8, mj)
            if i_old is not None:
                s = s_old[j * kb:(j + 1) * kb, :]
                p = jnp.exp2(s.reshape(kb // 8, 8, lanes) - m8_old[None]).reshape(kb, lanes).astype(BF16)
                vt = jnp.concatenate([v_ref[0, v_rows, a:a + kb], ones], axis=0)
                part = _dot(vt, p)
                acc = part if acc is None else acc + part
        if i_old is not None:
            o = acc[:HEAD_W] / acc[HEAD_W:HEAD_W + 1]
            if diff:
                od = o[:, :Q_TILE] - lam * o[:, Q_TILE:]
                ms = jnp.mean(od * od, axis=0, keepdims=True)
                y = od * lax.rsqrt(ms + EPS) * gain_ref[...] * (1.0 - lambda_init)
                o_ref[0, u_old, pl.ds(pl.multiple_of(h_old * HEAD_W, HEAD_W), HEAD_W), :] = y.astype(o_ref.dtype)
            else:
                y = jnp.concatenate([o[:, :Q_TILE], o[:, Q_TILE:]], axis=0)
                o_ref[0, u_old, pl.ds(pl.multiple_of(h_old * LANES, LANES), LANES), :] = y.astype(o_ref.dtype)
        return None if i_new is None else jnp.max(m8, axis=0, keepdims=True)

    n_units = n_kv * n_q
    assert n_units % 2 == 0

    def two_stages(t, m_even):
        m_odd = stage(2 * t + 1, s_odd, 2 * t, s_even, m_even)
        return stage(2 * t + 2, s_even, 2 * t + 1, s_odd, m_odd)

    m_even = lax.fori_loop(0, n_units // 2 - 1, two_stages, stage(0, s_even, None, None, None))
    m_odd = stage(n_units - 1, s_odd, n_units - 2, s_even, m_even)
    stage(None, None, n_units - 1, s_odd, m_odd)


def _attention(qt, k_parts, vt_parts, lam_p, gain, *, diff, lambda_init):
    b, n_q, qrows, _ = qt.shape
    n_kv = k_parts[0].shape[1]
    lens = [k.shape[2] for k in k_parts]
    assert qrows == n_kv * LANES and all(v.shape[1] == n_kv * HEAD_W for v in vt_parts)
    kb = KEY_BLOCK if all(n % KEY_BLOCK == 0 for n in lens) else math.gcd(*lens)
    kern = functools.partial(_attn_kernel, n_kv=n_kv, n_q=n_q, kb=kb, has_cache=len(k_parts) == 2, diff=diff,
                             lambda_init=lambda_init)
    kv = [a for pair in zip(k_parts, vt_parts) for a in pair]
    score_bytes = 2 * sum(lens) * 2 * Q_TILE * 4
    out_bytes = 2 * _vmem_bytes((n_q, 256, Q_TILE), BF16)
    operand_bytes = sum(_vmem_bytes(a.shape[1:], a.dtype) for a in [qt] + kv)
    double_buffered = score_bytes + out_bytes + 2 * operand_bytes <= VMEM_LIMIT_BYTES
    mode = {} if double_buffered else dict(pipeline_mode=pl.Buffered(1))
    whole = lambda a: pl.BlockSpec((1,) + a.shape[1:], lambda i: (i,) + (0,) * (a.ndim - 1), **mode)
    return pl.pallas_call(
        kern,
        out_shape=jax.ShapeDtypeStruct((b, n_q, 256, Q_TILE), BF16),
        grid=(b,),
        in_specs=[whole(qt)] + [whole(a) for a in kv] + [pl.BlockSpec(lam_p.shape, lambda i: (0, 0)),
                                                          pl.BlockSpec(gain.shape, lambda i: (0, 0))],
        out_specs=pl.BlockSpec((1, n_q, 256, Q_TILE), lambda i: (i, 0, 0, 0)),
        scratch_shapes=[pltpu.VMEM((sum(lens), 2 * Q_TILE), F32), pltpu.VMEM((sum(lens), 2 * Q_TILE), F32)],
        compiler_params=pltpu.CompilerParams(
            dimension_semantics=("parallel",), vmem_limit_bytes=VMEM_LIMIT_BYTES),
        name="diff_attn" if diff else "gqa_attn",
    )(qt, *kv, lam_p, gain)


def _retention_kernel(q_ref, kt_ref, v_ref, s0_ref, dec_ref, o_ref, sn_ref, kv_scr, sf_scr, sb_scr, *, n_chunks):
    c_len = Q_TILE
    dec = dec_ref[...]
    lg = jnp.minimum(dec, 0.0) - jnp.log1p(jnp.exp(-jnp.abs(dec)))
    row = lax.broadcasted_iota(jnp.int32, (c_len, c_len), 0)
    col = lax.broadcasted_iota(jnp.int32, (c_len, c_len), 1)
    rel = (row - col).astype(F32)
    pos = lax.broadcasted_iota(jnp.int32, (c_len, HEAD_W), 0).astype(F32)
    pos_t = lax.broadcasted_iota(jnp.int32, (HEAD_W, c_len), 1).astype(F32)
    masks, qdf, qdb, kdf, kdb, cdf, cdb = [], [], [], [], [], [], []
    for h in range(RET_HEADS):
        lf = lg[0:1, h:h + 1]
        lb = lg[1:2, h:h + 1]
        masks.append(jnp.where(rel >= 0, jnp.exp(lf * jnp.maximum(rel, 0.0)), 0.0)
                     + jnp.where(rel <= 0, jnp.exp(lb * jnp.maximum(-rel, 0.0)), 0.0))
        qdf.append(jnp.exp(lf * (pos + 1.0)))
        qdb.append(jnp.exp(lb * (c_len - pos)))
        kdf.append(jnp.exp(lf * (c_len - 1.0 - pos_t)))
        kdb.append(jnp.exp(lb * pos_t))
        cdf.append(jnp.exp(lf * c_len))
        cdb.append(jnp.exp(lb * c_len))

    def chunk_of(ref, c, h):
        start = pl.multiple_of(c * c_len, c_len)
        return ref[0, pl.ds(start, c_len), h * HEAD_W:(h + 1) * HEAD_W]

    unroll = 2 if n_chunks % 2 == 0 else 1

    def kv_body(c, carry):
        for h in range(RET_HEADS):
            kt = kt_ref[0, c, h * HEAD_W:(h + 1) * HEAD_W, :].astype(F32)
            vc = chunk_of(v_ref, c, h)
            kv_scr[c, h, 0] = _dot((kt * kdf[h]).astype(BF16), vc)
            kv_scr[c, h, 1] = _dot((kt * kdb[h]).astype(BF16), vc)
        return carry

    lax.fori_loop(0, n_chunks, kv_body, 0, unroll=unroll)

    def scan_body(i, states):
        sf, sb = states
        cb = n_chunks - 1 - i
        new_f, new_b = [], []
        for h in range(RET_HEADS):
            sf_scr[i, h] = sf[h]
            sb_scr[cb, h] = sb[h]
            new_f.append(sf[h] * cdf[h] + kv_scr[i, h, 0])
            new_b.append(sb[h] * cdb[h] + kv_scr[cb, h, 1])
        return tuple(new_f), tuple(new_b)

    sf_fin, sb_fin = lax.fori_loop(0, n_chunks, scan_body,
                                   (tuple(s0_ref[0, 0, h] for h in range(RET_HEADS)),
                                    tuple(s0_ref[0, 1, h] for h in range(RET_HEADS))))
    for h in range(RET_HEADS):
        sn_ref[0, 0, h] = sf_fin[h]
        sn_ref[0, 1, h] = sb_fin[h]

    def out_body(c, carry):
        outs = []
        for h in range(RET_HEADS):
            qc, vc = chunk_of(q_ref, c, h), chunk_of(v_ref, c, h)
            a = _dot(qc, kt_ref[0, c, h * HEAD_W:(h + 1) * HEAD_W, :]) * masks[h]
            o = _dot(a.astype(BF16), vc)
            o = o + _dot(qc, sf_scr[c, h].astype(BF16)) * qdf[h]
            o = o + _dot(qc, sb_scr[c, h].astype(BF16)) * qdb[h]
            ms = jnp.mean(o * o, axis=-1, keepdims=True)
            outs.append(o * lax.rsqrt(ms + EPS))
        start = pl.multiple_of(c * c_len, c_len)
        o_ref[0, pl.ds(start, c_len), :] = jnp.concatenate(outs, axis=-1).astype(o_ref.dtype)
        return carry

    lax.fori_loop(0, n_chunks, out_body, 0, unroll=unroll)


def _retention(q, kt, v, s0, decay):
    b, t, w = q.shape
    n_chunks = kt.shape[1]
    assert n_chunks * Q_TILE == t
    tok = pl.BlockSpec((1, t, w), lambda i: (i, 0, 0))
    st = pl.BlockSpec((1, 2, RET_HEADS, HEAD_W, HEAD_W), lambda i: (i, 0, 0, 0, 0))
    state_scr = pltpu.VMEM((n_chunks, RET_HEADS, HEAD_W, HEAD_W), F32)
    return pl.pallas_call(
        functools.partial(_retention_kernel, n_chunks=n_chunks),
        out_shape=[jax.ShapeDtypeStruct((b, t, w), BF16),
                   jax.ShapeDtypeStruct((b, 2, RET_HEADS, HEAD_W, HEAD_W), F32)],
        grid=(b,),
        in_specs=[tok, pl.BlockSpec((1, n_chunks, w, Q_TILE), lambda i: (i, 0, 0, 0)), tok, st,
                  pl.BlockSpec((2, RET_HEADS), lambda i: (0, 0))],
        out_specs=[tok, st],
        scratch_shapes=[pltpu.VMEM((n_chunks, RET_HEADS, 2, HEAD_W, HEAD_W), F32), state_scr, state_scr],
        compiler_params=pltpu.CompilerParams(
            dimension_semantics=("parallel",), vmem_limit_bytes=VMEM_LIMIT_BYTES),
        name="retention",
    )(q, kt, v, s0, decay)


def _merge_kernel(x_ref, mod_ref, ng_ref, oa_ref, prev_ref, next_ref, og_ref, yb_ref, yc_ref, yd_ref,
                  cw_ref, wg_ref, wbr_ref, wo_ref, fg_ref, o_ref, *, tm, last):
    j = pl.program_id(1)
    nb = x_ref.shape[0]

    def conv3(bi):
        g = oa_ref[bi, :, 256:512].astype(F32)
        row = lax.broadcasted_iota(jnp.int32, g.shape, 0)
        g_first = jnp.where(j == 0, 0.0, prev_ref[0, 7:8, 256:512].astype(F32))
        g_last = jnp.where(j == pl.num_programs(1) - 1, 0.0, next_ref[0, 0:1, 256:512].astype(F32))
        g_prev = jnp.where(row == 0, g_first, pltpu.roll(g, 1, 0))
        g_next = jnp.where(row == tm - 1, g_last, pltpu.roll(g, tm - 1, 0))
        return g_prev * cw_ref[0:1, :] + g * cw_ref[1:2, :] + g_next * cw_ref[2:3, :]

    convs = [conv3(bi) for bi in range(nb)]

    for bi, c in [(bi, c) for bi in range(nb) for c in range(tm // Q_TILE)]:
        rows = slice(c * Q_TILE, (c + 1) * Q_TILE)
        conv = convs[bi]
        x = x_ref[bi, rows, :]
        hb = _modulated_norm(x, mod_ref, ng_ref).astype(BF16)
        token_major = lambda ref: ref[bi, c].astype(F32).T.astype(BF16)
        ys = [(oa_ref[bi, rows, 0:256].astype(F32) * conv[rows]).astype(BF16),
              token_major(yb_ref) * og_ref[bi, rows, 0:256],
              yc_ref[bi, rows, :] * og_ref[bi, rows, 256:512],
              token_major(yd_ref) * og_ref[bi, rows, 512:768]]
        merged = None
        for i in range(N_BRANCH):
            gate = jax.nn.sigmoid(_dot(hb, wg_ref[:, i * 1024:(i + 1) * 1024]))
            term = gate * _dot(ys[i], wbr_ref[i])
            merged = term if merged is None else merged + term
        out = _dot(merged.astype(BF16), wo_ref[...])
        xn = x + mod_ref[0, 2:3, :] * out
        if last:
            ms = jnp.mean(xn * xn, axis=-1, keepdims=True)
            xn = xn * lax.rsqrt(ms + EPS) * fg_ref[...]
        o_ref[bi, rows, :] = xn


def _merge(x, mod, ng, oa, og, yb, yc, yd, conv_w, wg, wbr, wo, fg, *, layer, nb, tm, last):
    b, t, d = x.shape
    nt = t // tm
    rb = tm // 8
    per_batch_mod = mod.shape[0] > 1
    assert b % nb == 0 and (nb == 1 or (nt == 1 and not per_batch_mod))
    tok = lambda w: pl.BlockSpec((nb, tm, w), lambda i, j: (i, j, 0))
    q_tiles = pl.BlockSpec((nb, tm // Q_TILE, 256, Q_TILE), lambda i, j: (i, j, 0, 0))
    const = lambda shape: pl.BlockSpec(shape, lambda i, j: (0,) * len(shape), pipeline_mode=pl.Buffered(1))
    in_specs = [tok(d),
                pl.BlockSpec((1, 3, d), (lambda i, j: (i, 0, 0)) if per_batch_mod else (lambda i, j: (0, 0, 0))),
                const((1, d)), tok(512),
                pl.BlockSpec((1, 8, 512), lambda i, j: (i * nb, jnp.maximum(j * rb - 1, 0), 0)),
                pl.BlockSpec((1, 8, 512), lambda i, j: (i * nb, jnp.minimum((j + 1) * rb, t // 8 - 1), 0)),
                tok(768), q_tiles, tok(256), q_tiles,
                _layer_spec(conv_w, layer), _layer_spec(wg, layer), _layer_spec(wbr, layer), _layer_spec(wo, layer),
                const((1, d))]
    return pl.pallas_call(
        functools.partial(_merge_kernel, tm=tm, last=last),
        out_shape=jax.ShapeDtypeStruct((b, t, d), F32),
        grid=(b // nb, nt),
        in_specs=in_specs,
        out_specs=tok(d),
        compiler_params=pltpu.CompilerParams(
            dimension_semantics=("parallel", "parallel"), vmem_limit_bytes=VMEM_LIMIT_BYTES),
        name="merge",
    )(x, mod, ng, oa, oa, oa, og, yb, yc, yd, conv_w, wg, wbr, wo, fg)


def _widen_diff_q(w):
    lead = w.shape[:-1]
    dq = w[..., COL_D:COL_D + 256].reshape(lead + (DIFF_HEADS, 2, 1, DIFF_HEAD_DIM))
    eye = jnp.eye(2, dtype=w.dtype).reshape(2, 2, 1)
    return (dq * eye).reshape(lead + (2 * 256,)).astype(BF16)


def _rope_tables(n_tokens, head_dim):
    rows = n_tokens // GRID_W
    row = np.repeat(np.arange(rows, dtype=np.float64), GRID_W)
    col = np.tile(np.arange(GRID_W, dtype=np.float64), rows)
    n_axis = head_dim // 4
    inv_freq = ROPE_THETA ** (-np.arange(n_axis, dtype=np.float64) / n_axis)
    ang = np.concatenate([row[:, None] * inv_freq, col[:, None] * inv_freq], axis=-1)
    cos, sin = np.cos(ang), np.sin(ang)
    reps = LANES // head_dim
    return (np.tile(np.concatenate([cos, cos], axis=-1), (1, reps)).astype(np.float32),
            np.tile(np.concatenate([-sin, sin], axis=-1), (1, reps)).astype(np.float32))


def _token_tiling(b, t, shared_mod):
    if t % TOKEN_TILE == 0:
        return 1, TOKEN_TILE
    assert t % Q_TILE == 0
    nb = max(1, TOKEN_TILE // t) if shared_mod else 1
    while b % nb:
        nb -= 1
    return nb, t


def kernel(x_prompt, x_sample, cache_gqa_k, cache_gqa_v, cache_diff_k, cache_diff_v, state_ret, c, c_ctx,
           w_ada, b_ada, norm_gain, w_in, conv_w, gqa_q_gain, gqa_k_gain, ret_decay, diff_lambda,
           diff_norm_gain, w_branch, w_mgate, w_out, final_gain):
    depth = w_in.shape[0]
    bc, tc, d = x_prompt.shape
    bl, tl, _ = x_sample.shape

    cond = jnp.concatenate([c_ctx[None], c], axis=0)
    n_cond = cond.shape[0]
    cond = jnp.pad(cond, ((0, -n_cond % 8), (0, 0)))
    mod = _adaln(cond, w_ada.astype(BF16), b_ada)
    mod = mod.reshape(depth, cond.shape[0], 3, d)

    rope = _rope_tables(tl, HEAD_W) + _rope_tables(tl, DIFF_HEAD_DIM)
    fg = final_gain.reshape(1, d)
    zero_state = jnp.zeros((bc, 2, RET_HEADS, HEAD_W, HEAD_W), F32)

    def to_heads(cache):
        b, p = cache.shape[:2]
        return jnp.transpose(cache.reshape(b, p, -1, HEAD_W), (0, 2, 1, 3)).astype(BF16)

    def to_features(cache):
        b, p = cache.shape[:2]
        return jnp.transpose(cache.reshape(b, p, -1), (0, 2, 1)).astype(BF16)

    w_all, w_dq = w_in.astype(BF16), _widen_diff_q(w_in)
    wg, wbr, wo = w_mgate.astype(BF16), w_branch.astype(BF16), w_out.astype(BF16)

    xp, xs = x_prompt, x_sample
    new_caches = ([], [], [], [], [])
    for l in range(depth):
        lambda_init = 0.8 - 0.6 * math.exp(-0.3 * l)
        ng = norm_gain[l].reshape(1, d)
        qg = jnp.tile(gqa_q_gain[l], 256 // HEAD_W).reshape(1, 256)
        kg = jnp.tile(gqa_k_gain[l], 128 // HEAD_W).reshape(1, 128)
        dgain = diff_norm_gain[l].reshape(HEAD_W, 1)

        for latent in (False, True):
            x = xs if latent else xp
            b, t, _ = x.shape
            m = mod[l, 1:1 + bl] if latent else mod[l, 0:1]
            nb, tm = _token_tiling(b, t, shared_mod=not latent)
            outs = _inproj(x, m, ng, w_all, w_dq, qg, kg, rope, layer=l, latent=latent, nb=nb, tm=tm)
            oa, og, qb, kb, vb, qc, kc, vc, qd, kd, vd = outs[:11]
            kb, vb, kd, vd = [kb], [vb], [kd], [vd]
            if latent:
                kb.insert(0, to_heads(cache_gqa_k[:, l]))
                vb.insert(0, to_features(cache_gqa_v[:, l]))
                kd.insert(0, to_heads(cache_diff_k[:, l]))
                vd.insert(0, to_features(cache_diff_v[:, l]))
                s0 = state_ret[:, l]
            else:
                for lst, a in zip(new_caches[:4], outs[11:]):
                    lst.append(a)
                s0 = zero_state
            yb = _attention(qb, kb, vb, diff_lambda[l], dgain, diff=False, lambda_init=lambda_init)
            yd = _attention(qd, kd, vd, diff_lambda[l], dgain, diff=True, lambda_init=lambda_init)
            yc, s_new = _retention(qc, kc, vc, s0, ret_decay[l])
            if not latent:
                new_caches[4].append(s_new)
            x = _merge(x, m, ng, oa, og, yb, yc, yd, conv_w, wg, wbr, wo, fg, layer=l, nb=nb, tm=tm,
                       last=(l == depth - 1))
            if latent:
                xs = x
            else:
                xp = x

    ck = jnp.stack(new_caches[0], axis=1).reshape(bc, depth, tc, GQA_KV_HEADS, HEAD_W)
    cv = jnp.stack(new_caches[1], axis=1).reshape(bc, depth, tc, GQA_KV_HEADS, HEAD_W)
    cdk = jnp.stack(new_caches[2], axis=1).reshape(bc, depth, tc, DIFF_HEADS, 2, DIFF_HEAD_DIM)
    cdv = jnp.stack(new_caches[3], axis=1).reshape(bc, depth, tc, DIFF_HEADS, 2 * DIFF_HEAD_DIM)
    cs = jnp.stack(new_caches[4], axis=1)
    return (xp, xs, ck, cv, cdk, cdv, cs)
```

```python
import functools
import math

import jax
import jax.numpy as jnp
import numpy as np
from jax import lax
from jax.experimental import pallas as pl
from jax.experimental.pallas import tpu as pltpu

F32 = jnp.float32
BF16 = jnp.bfloat16

GRID_W = 64
BRANCH_W = 256
HEAD_W = 64
GQA_KV_HEADS = 2
RET_HEADS = 4
DIFF_HEADS = 4
DIFF_HEAD_DIM = 32
ROPE_THETA = 10000.0
EPS = 1e-6
LOG2_E = math.log2(math.e)
N_BRANCH = 4
LANES = 128
Q_TILE = 256
KEY_BLOCK = 256
TOKEN_TILE = 1024
VMEM_LIMIT_BYTES = 56 * 1024 * 1024

COL_A = 0
COL_B = 1024
COL_C = 1792
COL_D = 2816


def _vmem_bytes(shape, dtype):
    itemsize = jnp.dtype(dtype).itemsize
    sublanes = 8 * (4 // itemsize)
    rows = -(-shape[-2] // sublanes) * sublanes
    cols = -(-shape[-1] // LANES) * LANES
    return math.prod(shape[:-2]) * rows * cols * itemsize


def _silu(x):
    return x * jax.nn.sigmoid(x)


def _dot(a, b):
    return jnp.dot(a, b, preferred_element_type=F32)


def _group_ones(width):
    r = lax.broadcasted_iota(jnp.int32, (width, width), 0) >> 6
    c = lax.broadcasted_iota(jnp.int32, (width, width), 1) >> 6
    return jnp.where(r == c, 1.0, 0.0).astype(BF16)


def _head_rms(z, ones):
    zz = z * z
    hi = zz.astype(BF16)
    lo = (zz - hi.astype(F32)).astype(BF16)
    ss = _dot(hi, ones) + _dot(lo, ones)
    return z * lax.rsqrt(ss * (1.0 / HEAD_W) + EPS)


def _rope(x, cos_t, sin_t, half):
    lane = lax.broadcasted_iota(jnp.int32, (x.shape[0], LANES), 1)
    first = (lane & (2 * half - 1)) < half
    outs = []
    for j in range(x.shape[1] // LANES):
        xc = x[:, j * LANES:(j + 1) * LANES]
        partner = jnp.where(first, pltpu.roll(xc, LANES - half, 1), pltpu.roll(xc, half, 1))
        outs.append(xc * cos_t + partner * sin_t)
    return outs[0] if len(outs) == 1 else jnp.concatenate(outs, axis=1)


def _modulated_norm(x, mod_ref, ng_ref):
    ms = jnp.mean(x * x, axis=-1, keepdims=True)
    y = x * lax.rsqrt(ms + EPS) * ng_ref[...]
    return y * (1.0 + mod_ref[0, 1:2, :]) + mod_ref[0, 0:1, :]


def _adaln_kernel(cond_ref, w_ref, b_ref, o_ref):
    a = _silu(cond_ref[...]).astype(BF16)
    o_ref[0] = _dot(a, w_ref[0]) + b_ref[0]


def _adaln(cond, w_ada, b_ada):
    depth, d, d3 = w_ada.shape
    rows = cond.shape[0]
    return pl.pallas_call(
        _adaln_kernel,
        out_shape=jax.ShapeDtypeStruct((depth, rows, d3), F32),
        grid=(depth,),
        in_specs=[pl.BlockSpec((rows, d), lambda l: (0, 0)),
                  pl.BlockSpec((1, d, d3), lambda l: (l, 0, 0)),
                  pl.BlockSpec((1, 1, d3), lambda l: (l, 0, 0))],
        out_specs=pl.BlockSpec((1, rows, d3), lambda l: (l, 0, 0)),
        compiler_params=pltpu.CompilerParams(vmem_limit_bytes=VMEM_LIMIT_BYTES),
        name="adaln",
    )(cond, w_ada, b_ada.reshape(depth, 1, d3))


def _inproj_kernel(*refs, latent):
    if latent:
        (x_ref, mod_ref, ng_ref, w_ref, qg_ref, kg_ref, c64_ref, s64_ref, c32_ref, s32_ref,
         oa_ref, og_ref, qb_ref, kb_ref, vb_ref, qc_ref, kc_ref, vc_ref, qd_ref, kd_ref, vd_ref) = refs
    else:
        (x_ref, mod_ref, ng_ref, w_ref, qg_ref, kg_ref,
         oa_ref, og_ref, qb_ref, kb_ref, vb_ref, qc_ref, kc_ref, vc_ref, qd_ref, kd_ref, vd_ref,
         ck_ref, cv_ref, cdk_ref, cdv_ref) = refs
    nb, tm = x_ref.shape[:2]
    for bi, c in [(bi, c) for bi in range(nb) for c in range(tm // Q_TILE)]:
        rows = slice(c * Q_TILE, (c + 1) * Q_TILE)
        hb = _modulated_norm(x_ref[bi, rows, :], mod_ref, ng_ref).astype(BF16)

        z = _dot(hb, w_ref[:, COL_A:COL_A + 1024])
        oa_ref[bi, rows, 0:256] = (z[:, 0:256] * _silu(z[:, 768:1024])).astype(BF16)
        oa_ref[bi, rows, 256:512] = (z[:, 256:512] * z[:, 512:768]).astype(BF16)

        z = _dot(hb, w_ref[:, COL_B:COL_B + 768])
        q = _head_rms(z[:, 0:256], _group_ones(256)) * qg_ref[...]
        k = _head_rms(z[:, 256:384], _group_ones(128)) * kg_ref[...]
        v = z[:, 384:512]
        if latent:
            q = _rope(q, c64_ref[rows, :], s64_ref[rows, :], 32)
            k_att = _rope(k, c64_ref[rows, :], s64_ref[rows, :], 32)
        else:
            k_att = k
            ck_ref[bi, rows, :] = k
            cv_ref[bi, rows, :] = v
        qb_ref[bi, c] = (q * (HEAD_W ** -0.5 * LOG2_E)).T.astype(BF16)
        vb_ref[bi, :, rows] = v.T.astype(BF16)
        for h in range(GQA_KV_HEADS):
            kb_ref[bi, h, rows, :] = k_att[:, h * HEAD_W:(h + 1) * HEAD_W].astype(BF16)
        og_ref[bi, rows, 0:256] = _silu(z[:, 512:768]).astype(BF16)

        z = _dot(hb, w_ref[:, COL_C:COL_C + 1024])
        qc_ref[bi, rows, :] = z[:, 0:256].astype(BF16)
        kc_ref[bi, c] = (z[:, 256:512] * (HEAD_W ** -0.5)).T.astype(BF16)
        vc_ref[bi, rows, :] = z[:, 512:768].astype(BF16)
        og_ref[bi, rows, 256:512] = _silu(z[:, 768:1024]).astype(BF16)

        z = _dot(hb, w_ref[:, COL_D:COL_D + 1024])
        q = z[:, 0:256]
        k = z[:, 256:512]
        v = z[:, 512:768]
        if latent:
            q = _rope(q, c32_ref[rows, :], s32_ref[rows, :], 16)
            k_att = _rope(k, c32_ref[rows, :], s32_ref[rows, :], 16)
        else:
            k_att = k
            cdk_ref[bi, rows, :] = k
            cdv_ref[bi, rows, :] = v
        qt = (q * (DIFF_HEAD_DIM ** -0.5 * LOG2_E)).T.astype(BF16)
        dd = DIFF_HEAD_DIM
        zeros = jnp.zeros((2 * dd, Q_TILE), BF16)
        for h in range(DIFF_HEADS):
            src, dst = h * HEAD_W, h * LANES
            qd_ref[bi, c, dst:dst + dd, :] = qt[src:src + dd]
            qd_ref[bi, c, dst + dd:dst + 3 * dd, :] = zeros
            qd_ref[bi, c, dst + 3 * dd:dst + 4 * dd, :] = qt[src + dd:src + 2 * dd]
        vd_ref[bi, :, rows] = v.T.astype(BF16)
        for h in range(DIFF_HEADS):
            kd_ref[bi, h, rows, :] = k_att[:, h * HEAD_W:(h + 1) * HEAD_W].astype(BF16)
        og_ref[bi, rows, 512:768] = _silu(z[:, 768:1024]).astype(BF16)


def _layer_spec(a, layer):
    return pl.BlockSpec((None,) + a.shape[1:], lambda i, j: (layer,) + (0,) * (a.ndim - 1),
                        pipeline_mode=pl.Buffered(1))


def _inproj(x, mod, ng, w, qg, kg, rope_tabs, *, layer, latent, nb, tm):
    b, t, d = x.shape
    nt = t // tm
    per_batch_mod = mod.shape[0] > 1
    assert b % nb == 0 and not (per_batch_mod and nb > 1)
    tok = lambda w: pl.BlockSpec((nb, tm, w), lambda i, j: (i, j, 0))
    tok_t = lambda w: pl.BlockSpec((nb, w, tm), lambda i, j: (i, 0, j))
    heads = lambda n: pl.BlockSpec((nb, n, tm, HEAD_W), lambda i, j: (i, 0, j, 0))
    const = lambda shape: pl.BlockSpec(shape, lambda i, j: (0,) * len(shape), pipeline_mode=pl.Buffered(1))
    in_specs = [tok(d),
                pl.BlockSpec((1, 3, d), (lambda i, j: (i, 0, 0)) if per_batch_mod else (lambda i, j: (0, 0, 0))),
                const((1, d)), _layer_spec(w, layer), const((1, 256)), const((1, 128))]
    args = [x, mod, ng, w, qg, kg]
    if latent:
        in_specs += [pl.BlockSpec((tm, LANES), lambda i, j: (j, 0))] * 4
        args += list(rope_tabs)
    sd = jax.ShapeDtypeStruct
    qn = tm // Q_TILE
    q_tiles = lambda w: pl.BlockSpec((nb, qn, w, Q_TILE), lambda i, j: (i, j, 0, 0))
    out_shape = [sd((b, t, 512), BF16), sd((b, t, 768), BF16),
                 sd((b, t // Q_TILE, 256, Q_TILE), BF16), sd((b, GQA_KV_HEADS, t, HEAD_W), BF16), sd((b, 128, t), BF16),
                 sd((b, t, 256), BF16), sd((b, t // Q_TILE, 256, Q_TILE), BF16), sd((b, t, 256), BF16),
                 sd((b, t // Q_TILE, 512, Q_TILE), BF16), sd((b, DIFF_HEADS, t, HEAD_W), BF16), sd((b, 256, t), BF16)]
    out_specs = [tok(512), tok(768), q_tiles(256), heads(GQA_KV_HEADS), tok_t(128),
                 tok(256), q_tiles(256), tok(256), q_tiles(512), heads(DIFF_HEADS), tok_t(256)]
    if not latent:
        out_shape += [sd((b, t, 128), F32), sd((b, t, 128), F32), sd((b, t, 256), F32), sd((b, t, 256), F32)]
        out_specs += [tok(128), tok(128), tok(256), tok(256)]
    return pl.pallas_call(
        functools.partial(_inproj_kernel, latent=latent),
        out_shape=out_shape,
        grid=(b // nb, nt),
        in_specs=in_specs,
        out_specs=out_specs,
        compiler_params=pltpu.CompilerParams(
            dimension_semantics=("parallel", "parallel"), vmem_limit_bytes=VMEM_LIMIT_BYTES),
        name="inproj_lat" if latent else "inproj_ctx",
    )(*args)


def _attn_kernel(*refs, n_kv, n_q, kb, has_cache, diff, lambda_init):
    if has_cache:
        qt_ref, kc_ref, vc_ref, kn_ref, vn_ref, lam_ref, gain_ref, o_ref, s_even, s_odd = refs
        sources = [(kc_ref, vc_ref), (kn_ref, vn_ref)]
    else:
        qt_ref, kn_ref, vn_ref, lam_ref, gain_ref, o_ref, s_even, s_odd = refs
        sources = [(kn_ref, vn_ref)]
    blocks = [(k_ref, v_ref, a) for k_ref, v_ref in sources for a in range(0, k_ref.shape[2], kb)]
    lanes = 2 * Q_TILE
    ones = jnp.ones((16, kb), BF16)
    if diff:
        lp = lam_ref[...]
        lam = (jnp.exp(jnp.sum(lp[0:1] * lp[1:2], axis=-1, keepdims=True))
               - jnp.exp(jnp.sum(lp[2:3] * lp[3:4], axis=-1, keepdims=True)) + lambda_init)

    def stage(i_new, s_new, i_old, s_old, m_old):
        if i_new is not None:
            h_new, u_new = i_new // n_q, i_new % n_q
            q2 = qt_ref[0, u_new, pl.ds(pl.multiple_of(h_new * LANES, LANES), LANES), :]
            qt = jnp.concatenate([q2[:HEAD_W], q2[HEAD_W:]], axis=1)
        if i_old is not None:
            h_old, u_old = i_old // n_q, i_old % n_q
            v_rows = pl.ds(pl.multiple_of(h_old * HEAD_W, HEAD_W), HEAD_W)
            m8_old = jnp.broadcast_to(m_old, (8, lanes))
        m8, acc = None, None
        for j, (k_ref, v_ref, a) in enumerate(blocks):
            if i_new is not None:
                s = _dot(k_ref[0, h_new, a:a + kb, :], qt)
                s_new[j * kb:(j + 1) * kb, :] = s
                mj = jnp.max(s.reshape(kb // 8, 8, lanes), axis=0)
                m8 = mj if m8 is None else jnp.maximum(m8, mj)
            if i_old is not None:
                s = s_old[j * kb:(j + 1) * kb, :]
                p = jnp.exp2(s.reshape(kb // 8, 8, lanes) - m8_old[None]).reshape(kb, lanes).astype(BF16)
                vt = jnp.concatenate([v_ref[0, v_rows, a:a + kb], ones], axis=0)
                part = _dot(vt, p)
                acc = part if acc is None else acc + part
        if i_old is not None:
            o = acc[:HEAD_W] / acc[HEAD_W:HEAD_W + 1]
            if diff:
                od = o[:, :Q_TILE] - lam * o[:, Q_TILE:]
                ms = jnp.mean(od * od, axis=0, keepdims=True)
                y = od * lax.rsqrt(ms + EPS) * gain_ref[...] * (1.0 - lambda_init)
                o_ref[0, u_old, pl.ds(pl.multiple_of(h_old * HEAD_W, HEAD_W), HEAD_W), :] = y.astype(o_ref.dtype)
            else:
                y = jnp.concatenate([o[:, :Q_TILE], o[:, Q_TILE:]], axis=0)
                o_ref[0, u_old, pl.ds(pl.multiple_of(h_old * LANES, LANES), LANES), :] = y.astype(o_ref.dtype)
        return None if i_new is None else jnp.max(m8, axis=0, keepdims=True)

    n_units = n_kv * n_q
    assert n_units % 2 == 0

    def two_stages(t, m_even):
        m_odd = stage(2 * t + 1, s_odd, 2 * t, s_even, m_even)
        return stage(2 * t + 2, s_even, 2 * t + 1, s_odd, m_odd)

    n_pairs = n_units // 2 - 1
    m_even = stage(0, s_even, None, None, None)
    if n_pairs % 2:
        m_even = two_stages(0, m_even)
    m_even = lax.fori_loop(n_pairs % 2, n_pairs, two_stages, m_even, unroll=2)
    m_odd = stage(n_units - 1, s_odd, n_units - 2, s_even, m_even)
    stage(None, None, n_units - 1, s_odd, m_odd)


def _attention(qt, k_parts, vt_parts, lam_p, gain, *, diff, lambda_init):
    b, n_q, qrows, _ = qt.shape
    n_kv = k_parts[0].shape[1]
    lens = [k.shape[2] for k in k_parts]
    assert qrows == n_kv * LANES and all(v.shape[1] == n_kv * HEAD_W for v in vt_parts)
    kb = KEY_BLOCK if all(n % KEY_BLOCK == 0 for n in lens) else math.gcd(*lens)
    kern = functools.partial(_attn_kernel, n_kv=n_kv, n_q=n_q, kb=kb, has_cache=len(k_parts) == 2, diff=diff,
                             lambda_init=lambda_init)
    kv = [a for pair in zip(k_parts, vt_parts) for a in pair]
    score_bytes = 2 * sum(lens) * 2 * Q_TILE * 4
    out_bytes = 2 * _vmem_bytes((n_q, 256, Q_TILE), BF16)
    operand_bytes = sum(_vmem_bytes(a.shape[1:], a.dtype) for a in [qt] + kv)
    double_buffered = score_bytes + out_bytes + 2 * operand_bytes <= VMEM_LIMIT_BYTES
    mode = {} if double_buffered else dict(pipeline_mode=pl.Buffered(1))
    whole = lambda a: pl.BlockSpec((1,) + a.shape[1:], lambda i: (i,) + (0,) * (a.ndim - 1), **mode)
    return pl.pallas_call(
        kern,
        out_shape=jax.ShapeDtypeStruct((b, n_q, 256, Q_TILE), BF16),
        grid=(b,),
        in_specs=[whole(qt)] + [whole(a) for a in kv] + [pl.BlockSpec(lam_p.shape, lambda i: (0, 0)),
                                                          pl.BlockSpec(gain.shape, lambda i: (0, 0))],
        out_specs=pl.BlockSpec((1, n_q, 256, Q_TILE), lambda i: (i, 0, 0, 0)),
        scratch_shapes=[pltpu.VMEM((sum(lens), 2 * Q_TILE), F32), pltpu.VMEM((sum(lens), 2 * Q_TILE), F32)],
        compiler_params=pltpu.CompilerParams(
            dimension_semantics=("parallel",), vmem_limit_bytes=VMEM_LIMIT_BYTES),
        name="diff_attn" if diff else "gqa_attn",
    )(qt, *kv, lam_p, gain)


def _retention_kernel(q_ref, kt_ref, v_ref, s0_ref, dec_ref, o_ref, sn_ref, kv_scr, sf_scr, sb_scr, *, n_chunks):
    c_len = Q_TILE
    dec = dec_ref[...]
    lg = jnp.minimum(dec, 0.0) - jnp.log1p(jnp.exp(-jnp.abs(dec)))
    row = lax.broadcasted_iota(jnp.int32, (c_len, c_len), 0)
    col = lax.broadcasted_iota(jnp.int32, (c_len, c_len), 1)
    rel = (row - col).astype(F32)
    pos = lax.broadcasted_iota(jnp.int32, (c_len, HEAD_W), 0).astype(F32)
    pos_t = lax.broadcasted_iota(jnp.int32, (HEAD_W, c_len), 1).astype(F32)
    masks, qdf, qdb, kdf, kdb, cdf, cdb = [], [], [], [], [], [], []
    for h in range(RET_HEADS):
        lf = lg[0:1, h:h + 1]
        lb = lg[1:2, h:h + 1]
        masks.append(jnp.where(rel >= 0, jnp.exp(lf * jnp.maximum(rel, 0.0)), 0.0)
                     + jnp.where(rel <= 0, jnp.exp(lb * jnp.maximum(-rel, 0.0)), 0.0))
        qdf.append(jnp.exp(lf * (pos + 1.0)))
        qdb.append(jnp.exp(lb * (c_len - pos)))
        kdf.append(jnp.exp(lf * (c_len - 1.0 - pos_t)))
        kdb.append(jnp.exp(lb * pos_t))
        cdf.append(jnp.exp(lf * c_len))
        cdb.append(jnp.exp(lb * c_len))

    def chunk_of(ref, c, h):
        start = pl.multiple_of(c * c_len, c_len)
        return ref[0, pl.ds(start, c_len), h * HEAD_W:(h + 1) * HEAD_W]

    unroll = 2 if n_chunks % 2 == 0 else 1

    def kv_body(c, carry):
        for h in range(RET_HEADS):
            kt = kt_ref[0, c, h * HEAD_W:(h + 1) * HEAD_W, :].astype(F32)
            vc = chunk_of(v_ref, c, h)
            kv_scr[c, h, 0] = _dot((kt * kdf[h]).astype(BF16), vc)
            kv_scr[c, h, 1] = _dot((kt * kdb[h]).astype(BF16), vc)
        return carry

    lax.fori_loop(0, n_chunks, kv_body, 0, unroll=unroll)

    def scan_body(i, states):
        sf, sb = states
        cb = n_chunks - 1 - i
        new_f, new_b = [], []
        for h in range(RET_HEADS):
            sf_scr[i, h] = sf[h]
            sb_scr[cb, h] = sb[h]
            new_f.append(sf[h] * cdf[h] + kv_scr[i, h, 0])
            new_b.append(sb[h] * cdb[h] + kv_scr[cb, h, 1])
        return tuple(new_f), tuple(new_b)

    sf_fin, sb_fin = lax.fori_loop(0, n_chunks, scan_body,
                                   (tuple(s0_ref[0, 0, h] for h in range(RET_HEADS)),
                                    tuple(s0_ref[0, 1, h] for h in range(RET_HEADS))))
    for h in range(RET_HEADS):
        sn_ref[0, 0, h] = sf_fin[h]
        sn_ref[0, 1, h] = sb_fin[h]

    def out_body(c, carry):
        outs = []
        for h in range(RET_HEADS):
            qc, vc = chunk_of(q_ref, c, h), chunk_of(v_ref, c, h)
            a = _dot(qc, kt_ref[0, c, h * HEAD_W:(h + 1) * HEAD_W, :]) * masks[h]
            o = _dot(a.astype(BF16), vc)
            o = o + _dot(qc, sf_scr[c, h].astype(BF16)) * qdf[h]
            o = o + _dot(qc, sb_scr[c, h].astype(BF16)) * qdb[h]
            ms = jnp.mean(o * o, axis=-1, keepdims=True)
            outs.append(o * lax.rsqrt(ms + EPS))
        start = pl.multiple_of(c * c_len, c_len)
        o_ref[0, pl.ds(start, c_len), :] = jnp.concatenate(outs, axis=-1).astype(o_ref.dtype)
        return carry

    lax.fori_loop(0, n_chunks, out_body, 0, unroll=unroll)


def _retention(q, kt, v, s0, decay):
    b, t, w = q.shape
    n_chunks = kt.shape[1]
    assert n_chunks * Q_TILE == t
    tok = pl.BlockSpec((1, t, w), lambda i: (i, 0, 0))
    st = pl.BlockSpec((1, 2, RET_HEADS, HEAD_W, HEAD_W), lambda i: (i, 0, 0, 0, 0))
    state_scr = pltpu.VMEM((n_chunks, RET_HEADS, HEAD_W, HEAD_W), F32)
    return pl.pallas_call(
        functools.partial(_retention_kernel, n_chunks=n_chunks),
        out_shape=[jax.ShapeDtypeStruct((b, t, w), BF16),
                   jax.ShapeDtypeStruct((b, 2, RET_HEADS, HEAD_W, HEAD_W), F32)],
        grid=(b,),
        in_specs=[tok, pl.BlockSpec((1, n_chunks, w, Q_TILE), lambda i: (i, 0, 0, 0)), tok, st,
                  pl.BlockSpec((2, RET_HEADS), lambda i: (0, 0))],
        out_specs=[tok, st],
        scratch_shapes=[pltpu.VMEM((n_chunks, RET_HEADS, 2, HEAD_W, HEAD_W), F32), state_scr, state_scr],
        compiler_params=pltpu.CompilerParams(
            dimension_semantics=("parallel",), vmem_limit_bytes=VMEM_LIMIT_BYTES),
        name="retention",
    )(q, kt, v, s0, decay)


def _merge_kernel(x_ref, mod_ref, ng_ref, oa_ref, prev_ref, next_ref, og_ref, yb_ref, yc_ref, yd_ref,
                  cw_ref, wg_ref, wbr_ref, wo_ref, fg_ref, o_ref, *, tm, last):
    j = pl.program_id(1)
    nb = x_ref.shape[0]

    def conv3(bi):
        g = oa_ref[bi, :, 256:512].astype(F32)
        row = lax.broadcasted_iota(jnp.int32, g.shape, 0)
        g_first = jnp.where(j == 0, 0.0, prev_ref[0, 7:8, 256:512].astype(F32))
        g_last = jnp.where(j == pl.num_programs(1) - 1, 0.0, next_ref[0, 0:1, 256:512].astype(F32))
        g_prev = jnp.where(row == 0, g_first, pltpu.roll(g, 1, 0))
        g_next = jnp.where(row == tm - 1, g_last, pltpu.roll(g, tm - 1, 0))
        return g_prev * cw_ref[0:1, :] + g * cw_ref[1:2, :] + g_next * cw_ref[2:3, :]

    convs = [conv3(bi) for bi in range(nb)]

    for bi, c in [(bi, c) for bi in range(nb) for c in range(tm // Q_TILE)]:
        rows = slice(c * Q_TILE, (c + 1) * Q_TILE)
        conv = convs[bi]
        x = x_ref[bi, rows, :]
        hb = _modulated_norm(x, mod_ref, ng_ref).astype(BF16)
        token_major = lambda ref: ref[bi, c].astype(F32).T.astype(BF16)
        ys = [(oa_ref[bi, rows, 0:256].astype(F32) * conv[rows]).astype(BF16),
              token_major(yb_ref) * og_ref[bi, rows, 0:256],
              yc_ref[bi, rows, :] * og_ref[bi, rows, 256:512],
              token_major(yd_ref) * og_ref[bi, rows, 512:768]]
        merged = None
        for i in range(N_BRANCH):
            gate = jax.nn.sigmoid(_dot(hb, wg_ref[:, i * 1024:(i + 1) * 1024]))
            term = gate * _dot(ys[i], wbr_ref[i])
            merged = term if merged is None else merged + term
        out = _dot(merged.astype(BF16), wo_ref[...])
        xn = x + mod_ref[0, 2:3, :] * out
        if last:
            ms = jnp.mean(xn * xn, axis=-1, keepdims=True)
            xn = xn * lax.rsqrt(ms + EPS) * fg_ref[...]
        o_ref[bi, rows, :] = xn


def _merge(x, mod, ng, oa, og, yb, yc, yd, conv_w, wg, wbr, wo, fg, *, layer, nb, tm, last):
    b, t, d = x.shape
    nt = t // tm
    rb = tm // 8
    per_batch_mod = mod.shape[0] > 1
    assert b % nb == 0 and (nb == 1 or (nt == 1 and not per_batch_mod))
    tok = lambda w: pl.BlockSpec((nb, tm, w), lambda i, j: (i, j, 0))
    q_tiles = pl.BlockSpec((nb, tm // Q_TILE, 256, Q_TILE), lambda i, j: (i, j, 0, 0))
    const = lambda shape: pl.BlockSpec(shape, lambda i, j: (0,) * len(shape), pipeline_mode=pl.Buffered(1))
    in_specs = [tok(d),
                pl.BlockSpec((1, 3, d), (lambda i, j: (i, 0, 0)) if per_batch_mod else (lambda i, j: (0, 0, 0))),
                const((1, d)), tok(512),
                pl.BlockSpec((1, 8, 512), lambda i, j: (i * nb, jnp.maximum(j * rb - 1, 0), 0)),
                pl.BlockSpec((1, 8, 512), lambda i, j: (i * nb, jnp.minimum((j + 1) * rb, t // 8 - 1), 0)),
                tok(768), q_tiles, tok(256), q_tiles,
                _layer_spec(conv_w, layer), _layer_spec(wg, layer), _layer_spec(wbr, layer), _layer_spec(wo, layer),
                const((1, d))]
    return pl.pallas_call(
        functools.partial(_merge_kernel, tm=tm, last=last),
        out_shape=jax.ShapeDtypeStruct((b, t, d), F32),
        grid=(b // nb, nt),
        in_specs=in_specs,
        out_specs=tok(d),
        compiler_params=pltpu.CompilerParams(
            dimension_semantics=("parallel", "parallel"), vmem_limit_bytes=VMEM_LIMIT_BYTES),
        name="merge",
    )(x, mod, ng, oa, oa, oa, og, yb, yc, yd, conv_w, wg, wbr, wo, fg)


def _rope_tables(n_tokens, head_dim):
    rows = n_tokens // GRID_W
    row = np.repeat(np.arange(rows, dtype=np.float64), GRID_W)
    col = np.tile(np.arange(GRID_W, dtype=np.float64), rows)
    n_axis = head_dim // 4
    inv_freq = ROPE_THETA ** (-np.arange(n_axis, dtype=np.float64) / n_axis)
    ang = np.concatenate([row[:, None] * inv_freq, col[:, None] * inv_freq], axis=-1)
    cos, sin = np.cos(ang), np.sin(ang)
    reps = LANES // head_dim
    return (np.tile(np.concatenate([cos, cos], axis=-1), (1, reps)).astype(np.float32),
            np.tile(np.concatenate([-sin, sin], axis=-1), (1, reps)).astype(np.float32))


def _token_tiling(b, t, shared_mod):
    if t % TOKEN_TILE == 0:
        return 1, TOKEN_TILE
    assert t % Q_TILE == 0
    nb = max(1, TOKEN_TILE // t) if shared_mod else 1
    while b % nb:
        nb -= 1
    return nb, t


def kernel(x_prompt, x_sample, cache_gqa_k, cache_gqa_v, cache_diff_k, cache_diff_v, state_ret, c, c_ctx,
           w_ada, b_ada, norm_gain, w_in, conv_w, gqa_q_gain, gqa_k_gain, ret_decay, diff_lambda,
           diff_norm_gain, w_branch, w_mgate, w_out, final_gain):
    depth = w_in.shape[0]
    bc, tc, d = x_prompt.shape
    bl, tl, _ = x_sample.shape

    cond = jnp.concatenate([c_ctx[None], c], axis=0)
    n_cond = cond.shape[0]
    cond = jnp.pad(cond, ((0, -n_cond % 8), (0, 0)))
    mod = _adaln(cond, w_ada.astype(BF16), b_ada)
    mod = mod.reshape(depth, cond.shape[0], 3, d)

    rope = _rope_tables(tl, HEAD_W) + _rope_tables(tl, DIFF_HEAD_DIM)
    fg = final_gain.reshape(1, d)
    zero_state = jnp.zeros((bc, 2, RET_HEADS, HEAD_W, HEAD_W), F32)

    def to_heads(cache):
        b, p = cache.shape[:2]
        return jnp.transpose(cache.reshape(b, p, -1, HEAD_W), (0, 2, 1, 3)).astype(BF16)

    def to_features(cache):
        b, p = cache.shape[:2]
        return jnp.transpose(cache.reshape(b, p, -1), (0, 2, 1)).astype(BF16)

    w_all = w_in.astype(BF16)
    wg, wbr, wo = w_mgate.astype(BF16), w_branch.astype(BF16), w_out.astype(BF16)

    xp, xs = x_prompt, x_sample
    new_caches = ([], [], [], [], [])
    for l in range(depth):
        lambda_init = 0.8 - 0.6 * math.exp(-0.3 * l)
        ng = norm_gain[l].reshape(1, d)
        qg = jnp.tile(gqa_q_gain[l], 256 // HEAD_W).reshape(1, 256)
        kg = jnp.tile(gqa_k_gain[l], 128 // HEAD_W).reshape(1, 128)
        dgain = diff_norm_gain[l].reshape(HEAD_W, 1)

        for latent in (False, True):
            x = xs if latent else xp
            b, t, _ = x.shape
            m = mod[l, 1:1 + bl] if latent else mod[l, 0:1]
            nb, tm = _token_tiling(b, t, shared_mod=not latent)
            outs = _inproj(x, m, ng, w_all, qg, kg, rope, layer=l, latent=latent, nb=nb, tm=tm)
            oa, og, qb, kb, vb, qc, kc, vc, qd, kd, vd = outs[:11]
            kb, vb, kd, vd = [kb], [vb], [kd], [vd]
            if latent:
                kb.insert(0, to_heads(cache_gqa_k[:, l]))
                vb.insert(0, to_features(cache_gqa_v[:, l]))
                kd.insert(0, to_heads(cache_diff_k[:, l]))
                vd.insert(0, to_features(cache_diff_v[:, l]))
                s0 = state_ret[:, l]
            else:
                for lst, a in zip(new_caches[:4], outs[11:]):
                    lst.append(a)
                s0 = zero_state
            yb = _attention(qb, kb, vb, diff_lambda[l], dgain, diff=False, lambda_init=lambda_init)
            yd = _attention(qd, kd, vd, diff_lambda[l], dgain, diff=True, lambda_init=lambda_init)
            yc, s_new = _retention(qc, kc, vc, s0, ret_decay[l])
            if not latent:
                new_caches[4].append(s_new)
            x = _merge(x, m, ng, oa, og, yb, yc, yd, conv_w, wg, wbr, wo, fg, layer=l, nb=nb, tm=tm,
                       last=(l == depth - 1))
            if latent:
                xs = x
            else:
                xp = x

    ck = jnp.stack(new_caches[0], axis=1).reshape(bc, depth, tc, GQA_KV_HEADS, HEAD_W)
    cv = jnp.stack(new_caches[1], axis=1).reshape(bc, depth, tc, GQA_KV_HEADS, HEAD_W)
    cdk = jnp.stack(new_caches[2], axis=1).reshape(bc, depth, tc, DIFF_HEADS, 2, DIFF_HEAD_DIM)
    cdv = jnp.stack(new_caches[3], axis=1).reshape(bc, depth, tc, DIFF_HEADS, 2 * DIFF_HEAD_DIM)
    cs = jnp.stack(new_caches[4], axis=1)
    return (xp, xs, ck, cv, cdk, cdv, cs)
```

```python
import functools
import math

import jax
import jax.numpy as jnp
import numpy as np
from jax import lax
from jax.experimental import pallas as pl
from jax.experimental.pallas import tpu as pltpu

F32 = jnp.float32
BF16 = jnp.bfloat16

GRID_W = 64
BRANCH_W = 256
HEAD_W = 64
GQA_KV_HEADS = 2
RET_HEADS = 4
DIFF_HEADS = 4
DIFF_HEAD_DIM = 32
ROPE_THETA = 10000.0
EPS = 1e-6
LOG2_E = math.log2(math.e)
N_BRANCH = 4
LANES = 128
Q_TILE = 256
KEY_BLOCK = 256
TOKEN_TILE = 1024
VMEM_LIMIT_BYTES = 56 * 1024 * 1024

COL_A = 0
COL_B = 1024
COL_C = 1792
COL_D = 2816


def _vmem_bytes(shape, dtype):
    itemsize = jnp.dtype(dtype).itemsize
    sublanes = 8 * (4 // itemsize)
    rows = -(-shape[-2] // sublanes) * sublanes
    cols = -(-shape[-1] // LANES) * LANES
    return math.prod(shape[:-2]) * rows * cols * itemsize


def _silu(x):
    return x * jax.nn.sigmoid(x)


def _dot(a, b):
    return jnp.dot(a, b, preferred_element_type=F32)


def _group_ones(width):
    r = lax.broadcasted_iota(jnp.int32, (width, width), 0) >> 6
    c = lax.broadcasted_iota(jnp.int32, (width, width), 1) >> 6
    return jnp.where(r == c, 1.0, 0.0).astype(BF16)


def _head_rms(z, ones):
    zz = z * z
    hi = zz.astype(BF16)
    lo = (zz - hi.astype(F32)).astype(BF16)
    ss = _dot(hi, ones) + _dot(lo, ones)
    return z * lax.rsqrt(ss * (1.0 / HEAD_W) + EPS)


def _rope(x, cos_t, sin_t, half):
    lane = lax.broadcasted_iota(jnp.int32, (x.shape[0], LANES), 1)
    first = (lane & (2 * half - 1)) < half
    outs = []
    for j in range(x.shape[1] // LANES):
        xc = x[:, j * LANES:(j + 1) * LANES]
        partner = jnp.where(first, pltpu.roll(xc, LANES - half, 1), pltpu.roll(xc, half, 1))
        outs.append(xc * cos_t + partner * sin_t)
    return outs[0] if len(outs) == 1 else jnp.concatenate(outs, axis=1)


def _modulated_norm(x, mod_ref, ng_ref):
    ms = jnp.mean(x * x, axis=-1, keepdims=True)
    y = x * lax.rsqrt(ms + EPS) * ng_ref[...]
    return y * (1.0 + mod_ref[0, 1:2, :]) + mod_ref[0, 0:1, :]


def _adaln_kernel(cond_ref, w_ref, b_ref, o_ref):
    a = _silu(cond_ref[...]).astype(BF16)
    o_ref[0] = _dot(a, w_ref[0]) + b_ref[0]


def _adaln(cond, w_ada, b_ada):
    depth, d, d3 = w_ada.shape
    rows = cond.shape[0]
    return pl.pallas_call(
        _adaln_kernel,
        out_shape=jax.ShapeDtypeStruct((depth, rows, d3), F32),
        grid=(depth,),
        in_specs=[pl.BlockSpec((rows, d), lambda l: (0, 0)),
                  pl.BlockSpec((1, d, d3), lambda l: (l, 0, 0)),
                  pl.BlockSpec((1, 1, d3), lambda l: (l, 0, 0))],
        out_specs=pl.BlockSpec((1, rows, d3), lambda l: (l, 0, 0)),
        compiler_params=pltpu.CompilerParams(vmem_limit_bytes=VMEM_LIMIT_BYTES),
        name="adaln",
    )(cond, w_ada, b_ada.reshape(depth, 1, d3))


def _inproj_kernel(*refs, latent):
    if latent:
        (x_ref, mod_ref, ng_ref, w_ref, qg_ref, kg_ref, c64_ref, s64_ref, c32_ref, s32_ref,
         oa_ref, og_ref, qb_ref, kb_ref, vb_ref, qc_ref, kc_ref, vc_ref, qd_ref, kd_ref, vd_ref) = refs
    else:
        (x_ref, mod_ref, ng_ref, w_ref, qg_ref, kg_ref,
         oa_ref, og_ref, qb_ref, kb_ref, vb_ref, qc_ref, kc_ref, vc_ref, qd_ref, kd_ref, vd_ref,
         ck_ref, cv_ref, cdk_ref, cdv_ref) = refs
    nb, tm = x_ref.shape[:2]
    for bi, c in [(bi, c) for bi in range(nb) for c in range(tm // Q_TILE)]:
        rows = slice(c * Q_TILE, (c + 1) * Q_TILE)
        hb = _modulated_norm(x_ref[bi, rows, :], mod_ref, ng_ref).astype(BF16)

        z = _dot(hb, w_ref[:, COL_A:COL_A + 1024])
        oa_ref[bi, rows, 0:256] = (z[:, 0:256] * _silu(z[:, 768:1024])).astype(BF16)
        oa_ref[bi, rows, 256:512] = (z[:, 256:512] * z[:, 512:768]).astype(BF16)

        z = _dot(hb, w_ref[:, COL_B:COL_B + 768])
        q = _head_rms(z[:, 0:256], _group_ones(256)) * qg_ref[...]
        k = _head_rms(z[:, 256:384], _group_ones(128)) * kg_ref[...]
        v = z[:, 384:512]
        if latent:
            q = _rope(q, c64_ref[rows, :], s64_ref[rows, :], 32)
            k_att = _rope(k, c64_ref[rows, :], s64_ref[rows, :], 32)
        else:
            k_att = k
            ck_ref[bi, rows, :] = k
            cv_ref[bi, rows, :] = v
        qb_ref[bi, c] = (q * (HEAD_W ** -0.5 * LOG2_E)).T.astype(BF16)
        vb_ref[bi, :, rows] = v.T.astype(BF16)
        for h in range(GQA_KV_HEADS):
            kb_ref[bi, h, rows, :] = k_att[:, h * HEAD_W:(h + 1) * HEAD_W].astype(BF16)
        og_ref[bi, rows, 0:256] = _silu(z[:, 512:768]).astype(BF16)

        z = _dot(hb, w_ref[:, COL_C:COL_C + 1024])
        qc_ref[bi, rows, :] = z[:, 0:256].astype(BF16)
        kc_ref[bi, c] = (z[:, 256:512] * (HEAD_W ** -0.5)).T.astype(BF16)
        vc_ref[bi, rows, :] = z[:, 512:768].astype(BF16)
        og_ref[bi, rows, 256:512] = _silu(z[:, 768:1024]).astype(BF16)

        z = _dot(hb, w_ref[:, COL_D:COL_D + 1024])
        q = z[:, 0:256]
        k = z[:, 256:512]
        v = z[:, 512:768]
        if latent:
            q = _rope(q, c32_ref[rows, :], s32_ref[rows, :], 16)
            k_att = _rope(k, c32_ref[rows, :], s32_ref[rows, :], 16)
        else:
            k_att = k
            cdk_ref[bi, rows, :] = k
            cdv_ref[bi, rows, :] = v
        qt = (q * (DIFF_HEAD_DIM ** -0.5 * LOG2_E)).T.astype(BF16)
        dd = DIFF_HEAD_DIM
        zeros = jnp.zeros((2 * dd, Q_TILE), BF16)
        for h in range(DIFF_HEADS):
            src, dst = h * HEAD_W, h * LANES
            qd_ref[bi, c, dst:dst + dd, :] = qt[src:src + dd]
            qd_ref[bi, c, dst + dd:dst + 3 * dd, :] = zeros
            qd_ref[bi, c, dst + 3 * dd:dst + 4 * dd, :] = qt[src + dd:src + 2 * dd]
        vd_ref[bi, :, rows] = v.T.astype(BF16)
        for h in range(DIFF_HEADS):
            kd_ref[bi, h, rows, :] = k_att[:, h * HEAD_W:(h + 1) * HEAD_W].astype(BF16)
        og_ref[bi, rows, 512:768] = _silu(z[:, 768:1024]).astype(BF16)


def _layer_spec(a, layer):
    return pl.BlockSpec((None,) + a.shape[1:], lambda i, j: (layer,) + (0,) * (a.ndim - 1),
                        pipeline_mode=pl.Buffered(1))


def _inproj(x, mod, ng, w, qg, kg, rope_tabs, *, layer, latent, nb, tm):
    b, t, d = x.shape
    nt = t // tm
    per_batch_mod = mod.shape[0] > 1
    assert b % nb == 0 and not (per_batch_mod and nb > 1)
    tok = lambda w: pl.BlockSpec((nb, tm, w), lambda i, j: (i, j, 0))
    tok_t = lambda w: pl.BlockSpec((nb, w, tm), lambda i, j: (i, 0, j))
    heads = lambda n: pl.BlockSpec((nb, n, tm, HEAD_W), lambda i, j: (i, 0, j, 0))
    const = lambda shape: pl.BlockSpec(shape, lambda i, j: (0,) * len(shape), pipeline_mode=pl.Buffered(1))
    in_specs = [tok(d),
                pl.BlockSpec((1, 3, d), (lambda i, j: (i, 0, 0)) if per_batch_mod else (lambda i, j: (0, 0, 0))),
                const((1, d)), _layer_spec(w, layer), const((1, 256)), const((1, 128))]
    args = [x, mod, ng, w, qg, kg]
    if latent:
        in_specs += [pl.BlockSpec((tm, LANES), lambda i, j: (j, 0))] * 4
        args += list(rope_tabs)
    sd = jax.ShapeDtypeStruct
    qn = tm // Q_TILE
    q_tiles = lambda w: pl.BlockSpec((nb, qn, w, Q_TILE), lambda i, j: (i, j, 0, 0))
    out_shape = [sd((b, t, 512), BF16), sd((b, t, 768), BF16),
                 sd((b, t // Q_TILE, 256, Q_TILE), BF16), sd((b, GQA_KV_HEADS, t, HEAD_W), BF16), sd((b, 128, t), BF16),
                 sd((b, t, 256), BF16), sd((b, t // Q_TILE, 256, Q_TILE), BF16), sd((b, t, 256), BF16),
                 sd((b, t // Q_TILE, 512, Q_TILE), BF16), sd((b, DIFF_HEADS, t, HEAD_W), BF16), sd((b, 256, t), BF16)]
    out_specs = [tok(512), tok(768), q_tiles(256), heads(GQA_KV_HEADS), tok_t(128),
                 tok(256), q_tiles(256), tok(256), q_tiles(512), heads(DIFF_HEADS), tok_t(256)]
    if not latent:
        out_shape += [sd((b, t, 128), F32), sd((b, t, 128), F32), sd((b, t, 256), F32), sd((b, t, 256), F32)]
        out_specs += [tok(128), tok(128), tok(256), tok(256)]
    return pl.pallas_call(
        functools.partial(_inproj_kernel, latent=latent),
        out_shape=out_shape,
        grid=(b // nb, nt),
        in_specs=in_specs,
        out_specs=out_specs,
        compiler_params=pltpu.CompilerParams(
            dimension_semantics=("parallel", "parallel"), vmem_limit_bytes=VMEM_LIMIT_BYTES),
        name="inproj_lat" if latent else "inproj_ctx",
    )(*args)


def _attn_kernel(*refs, n_kv, n_q, kb, has_cache, diff, lambda_init):
    if has_cache:
        qt_ref, kc_ref, vc_ref, kn_ref, vn_ref, lam_ref, gain_ref, o_ref, s_even, s_odd = refs
        sources = [(kc_ref, vc_ref), (kn_ref, vn_ref)]
    else:
        qt_ref, kn_ref, vn_ref, lam_ref, gain_ref, o_ref, s_even, s_odd = refs
        sources = [(kn_ref, vn_ref)]
    blocks = [(k_ref, v_ref, a) for k_ref, v_ref in sources for a in range(0, k_ref.shape[2], kb)]
    lanes = 2 * Q_TILE
    ones = jnp.ones((16, kb), BF16)
    if diff:
        lp = lam_ref[...]
        lam = (jnp.exp(jnp.sum(lp[0:1] * lp[1:2], axis=-1, keepdims=True))
               - jnp.exp(jnp.sum(lp[2:3] * lp[3:4], axis=-1, keepdims=True)) + lambda_init)

    def stage(i_new, s_new, i_old, s_old, m_old):
        if i_new is not None:
            h_new, u_new = i_new // n_q, i_new % n_q
            q2 = qt_ref[0, u_new, pl.ds(pl.multiple_of(h_new * LANES, LANES), LANES), :]
            qt = jnp.concatenate([q2[:HEAD_W], q2[HEAD_W:]], axis=1)
        if i_old is not None:
            h_old, u_old = i_old // n_q, i_old % n_q
            v_rows = pl.ds(pl.multiple_of(h_old * HEAD_W, HEAD_W), HEAD_W)
            m8_old = jnp.broadcast_to(m_old, (8, lanes))
        m8, acc = None, None
        for j, (k_ref, v_ref, a) in enumerate(blocks):
            if i_new is not None:
                s = _dot(k_ref[0, h_new, a:a + kb, :], qt)
                s_new[j * kb:(j + 1) * kb, :] = s
                mj = jnp.max(s.reshape(kb // 8, 8, lanes), axis=0)
                m8 = mj if m8 is None else jnp.maximum(m8, mj)
            if i_old is not None:
                s = s_old[j * kb:(j + 1) * kb, :]
                p = jnp.exp2(s.reshape(kb // 8, 8, lanes) - m8_old[None]).reshape(kb, lanes).astype(BF16)
                vt = jnp.concatenate([v_ref[0, v_rows, a:a + kb], ones], axis=0)
                part = _dot(vt, p)
                acc = part if acc is None else acc + part
        if i_old is not None:
            o = acc[:HEAD_W] / acc[HEAD_W:HEAD_W + 1]
            if diff:
                od = o[:, :Q_TILE] - lam * o[:, Q_TILE:]
                ms = jnp.mean(od * od, axis=0, keepdims=True)
                y = od * lax.rsqrt(ms + EPS) * gain_ref[...] * (1.0 - lambda_init)
                o_ref[0, u_old, pl.ds(pl.multiple_of(h_old * HEAD_W, HEAD_W), HEAD_W), :] = y.astype(o_ref.dtype)
            else:
                y = jnp.concatenate([o[:, :Q_TILE], o[:, Q_TILE:]], axis=0)
                o_ref[0, u_old, pl.ds(pl.multiple_of(h_old * LANES, LANES), LANES), :] = y.astype(o_ref.dtype)
        return None if i_new is None else jnp.max(m8, axis=0, keepdims=True)

    n_units = n_kv * n_q
    assert n_units % 2 == 0

    def two_stages(t, m_even):
        m_odd = stage(2 * t + 1, s_odd, 2 * t, s_even, m_even)
        return stage(2 * t + 2, s_even, 2 * t + 1, s_odd, m_odd)

    n_pairs = n_units // 2 - 1
    m_even = stage(0, s_even, None, None, None)
    if n_pairs % 2:
        m_even = two_stages(0, m_even)
    m_even = lax.fori_loop(n_pairs % 2, n_pairs, two_stages, m_even, unroll=2)
    m_odd = stage(n_units - 1, s_odd, n_units - 2, s_even, m_even)
    stage(None, None, n_units - 1, s_odd, m_odd)


def _attention(qt, k_parts, vt_parts, lam_p, gain, *, diff, lambda_init):
    b, n_q, qrows, _ = qt.shape
    n_kv = k_parts[0].shape[1]
    lens = [k.shape[2] for k in k_parts]
    assert qrows == n_kv * LANES and all(v.shape[1] == n_kv * HEAD_W for v in vt_parts)
    kb = KEY_BLOCK if all(n % KEY_BLOCK == 0 for n in lens) else math.gcd(*lens)
    kern = functools.partial(_attn_kernel, n_kv=n_kv, n_q=n_q, kb=kb, has_cache=len(k_parts) == 2, diff=diff,
                             lambda_init=lambda_init)
    kv = [a for pair in zip(k_parts, vt_parts) for a in pair]
    score_bytes = 2 * sum(lens) * 2 * Q_TILE * 4
    out_bytes = 2 * _vmem_bytes((n_q, 256, Q_TILE), BF16)
    operand_bytes = sum(_vmem_bytes(a.shape[1:], a.dtype) for a in [qt] + kv)
    double_buffered = score_bytes + out_bytes + 2 * operand_bytes <= VMEM_LIMIT_BYTES
    mode = {} if double_buffered else dict(pipeline_mode=pl.Buffered(1))
    whole = lambda a: pl.BlockSpec((1,) + a.shape[1:], lambda i: (i,) + (0,) * (a.ndim - 1), **mode)
    return pl.pallas_call(
        kern,
        out_shape=jax.ShapeDtypeStruct((b, n_q, 256, Q_TILE), BF16),
        grid=(b,),
        in_specs=[whole(qt)] + [whole(a) for a in kv] + [pl.BlockSpec(lam_p.shape, lambda i: (0, 0)),
                                                          pl.BlockSpec(gain.shape, lambda i: (0, 0))],
        out_specs=pl.BlockSpec((1, n_q, 256, Q_TILE), lambda i: (i, 0, 0, 0)),
        scratch_shapes=[pltpu.VMEM((sum(lens), 2 * Q_TILE), F32), pltpu.VMEM((sum(lens), 2 * Q_TILE), F32)],
        compiler_params=pltpu.CompilerParams(
            dimension_semantics=("parallel",), vmem_limit_bytes=VMEM_LIMIT_BYTES),
        name="diff_attn" if diff else "gqa_attn",
    )(qt, *kv, lam_p, gain)


def _retention_kernel(q_ref, kt_ref, v_ref, s0_ref, dec_ref, o_ref, sn_ref, kv_scr, sf_scr, sb_scr,
                      mask_scr, qd_scr, kd_scr, cd_scr, *, n_chunks):
    c_len = Q_TILE

    @pl.when(pl.program_id(0) == 0)
    def _fill_decay_tables():
        dec = dec_ref[...]
        lg = jnp.minimum(dec, 0.0) - jnp.log1p(jnp.exp(-jnp.abs(dec)))
        row = lax.broadcasted_iota(jnp.int32, (c_len, c_len), 0)
        col = lax.broadcasted_iota(jnp.int32, (c_len, c_len), 1)
        rel = (row - col).astype(F32)
        pos = lax.broadcasted_iota(jnp.int32, (c_len, HEAD_W), 0).astype(F32)
        pos_t = lax.broadcasted_iota(jnp.int32, (HEAD_W, c_len), 1).astype(F32)
        for h in range(RET_HEADS):
            lf = lg[0:1, h:h + 1]
            lb = lg[1:2, h:h + 1]
            mask_scr[h] = (jnp.where(rel >= 0, jnp.exp(lf * jnp.maximum(rel, 0.0)), 0.0)
                           + jnp.where(rel <= 0, jnp.exp(lb * jnp.maximum(-rel, 0.0)), 0.0))
            qd_scr[h, 0] = jnp.exp(lf * (pos + 1.0))
            qd_scr[h, 1] = jnp.exp(lb * (c_len - pos))
            kd_scr[h, 0] = jnp.exp(lf * (c_len - 1.0 - pos_t))
            kd_scr[h, 1] = jnp.exp(lb * pos_t)
            cd_scr[h, 0] = jnp.broadcast_to(jnp.exp(lf * c_len), (HEAD_W, HEAD_W))
            cd_scr[h, 1] = jnp.broadcast_to(jnp.exp(lb * c_len), (HEAD_W, HEAD_W))

    def chunk_of(ref, c, h):
        start = pl.multiple_of(c * c_len, c_len)
        return ref[0, pl.ds(start, c_len), h * HEAD_W:(h + 1) * HEAD_W]

    unroll = 2 if n_chunks % 2 == 0 else 1

    def kv_body(c, carry):
        for h in range(RET_HEADS):
            kt = kt_ref[0, c, h * HEAD_W:(h + 1) * HEAD_W, :].astype(F32)
            vc = chunk_of(v_ref, c, h)
            kv_scr[c, h, 0] = _dot((kt * kd_scr[h, 0]).astype(BF16), vc)
            kv_scr[c, h, 1] = _dot((kt * kd_scr[h, 1]).astype(BF16), vc)
        return carry

    lax.fori_loop(0, n_chunks, kv_body, 0, unroll=unroll)

    def scan_body(i, states):
        sf, sb = states
        cb = n_chunks - 1 - i
        new_f, new_b = [], []
        for h in range(RET_HEADS):
            sf_scr[i, h] = sf[h]
            sb_scr[cb, h] = sb[h]
            new_f.append(sf[h] * cd_scr[h, 0] + kv_scr[i, h, 0])
            new_b.append(sb[h] * cd_scr[h, 1] + kv_scr[cb, h, 1])
        return tuple(new_f), tuple(new_b)

    sf_fin, sb_fin = lax.fori_loop(0, n_chunks, scan_body,
                                   (tuple(s0_ref[0, 0, h] for h in range(RET_HEADS)),
                                    tuple(s0_ref[0, 1, h] for h in range(RET_HEADS))))
    for h in range(RET_HEADS):
        sn_ref[0, 0, h] = sf_fin[h]
        sn_ref[0, 1, h] = sb_fin[h]

    def out_body(c, carry):
        outs = []
        for h in range(RET_HEADS):
            qc, vc = chunk_of(q_ref, c, h), chunk_of(v_ref, c, h)
            a = _dot(qc, kt_ref[0, c, h * HEAD_W:(h + 1) * HEAD_W, :]) * mask_scr[h]
            o = _dot(a.astype(BF16), vc)
            o = o + _dot(qc, sf_scr[c, h].astype(BF16)) * qd_scr[h, 0]
            o = o + _dot(qc, sb_scr[c, h].astype(BF16)) * qd_scr[h, 1]
            ms = jnp.mean(o * o, axis=-1, keepdims=True)
            outs.append(o * lax.rsqrt(ms + EPS))
        start = pl.multiple_of(c * c_len, c_len)
        o_ref[0, pl.ds(start, c_len), :] = jnp.concatenate(outs, axis=-1).astype(o_ref.dtype)
        return carry

    lax.fori_loop(0, n_chunks, out_body, 0, unroll=unroll)


def _retention(q, kt, v, s0, decay):
    b, t, w = q.shape
    n_chunks = kt.shape[1]
    assert n_chunks * Q_TILE == t
    tok = pl.BlockSpec((1, t, w), lambda i: (i, 0, 0))
    st = pl.BlockSpec((1, 2, RET_HEADS, HEAD_W, HEAD_W), lambda i: (i, 0, 0, 0, 0))
    state_scr = pltpu.VMEM((n_chunks, RET_HEADS, HEAD_W, HEAD_W), F32)
    return pl.pallas_call(
        functools.partial(_retention_kernel, n_chunks=n_chunks),
        out_shape=[jax.ShapeDtypeStruct((b, t, w), BF16),
                   jax.ShapeDtypeStruct((b, 2, RET_HEADS, HEAD_W, HEAD_W), F32)],
        grid=(b,),
        in_specs=[tok, pl.BlockSpec((1, n_chunks, w, Q_TILE), lambda i: (i, 0, 0, 0)), tok, st,
                  pl.BlockSpec((2, RET_HEADS), lambda i: (0, 0))],
        out_specs=[tok, st],
        scratch_shapes=[pltpu.VMEM((n_chunks, RET_HEADS, 2, HEAD_W, HEAD_W), F32), state_scr, state_scr,
                        pltpu.VMEM((RET_HEADS, Q_TILE, Q_TILE), F32),
                        pltpu.VMEM((RET_HEADS, 2, Q_TILE, HEAD_W), F32),
                        pltpu.VMEM((RET_HEADS, 2, HEAD_W, Q_TILE), F32),
                        pltpu.VMEM((RET_HEADS, 2, HEAD_W, HEAD_W), F32)],
        compiler_params=pltpu.CompilerParams(
            dimension_semantics=("arbitrary",), vmem_limit_bytes=VMEM_LIMIT_BYTES),
        name="retention",
    )(q, kt, v, s0, decay)


def _merge_kernel(x_ref, mod_ref, ng_ref, oa_ref, prev_ref, next_ref, og_ref, yb_ref, yc_ref, yd_ref,
                  cw_ref, wg_ref, wbr_ref, wo_ref, fg_ref, o_ref, *, tm, last):
    j = pl.program_id(1)
    nb = x_ref.shape[0]

    def conv3(bi):
        g = oa_ref[bi, :, 256:512].astype(F32)
        row = lax.broadcasted_iota(jnp.int32, g.shape, 0)
        g_first = jnp.where(j == 0, 0.0, prev_ref[0, 7:8, 256:512].astype(F32))
        g_last = jnp.where(j == pl.num_programs(1) - 1, 0.0, next_ref[0, 0:1, 256:512].astype(F32))
        g_prev = jnp.where(row == 0, g_first, pltpu.roll(g, 1, 0))
        g_next = jnp.where(row == tm - 1, g_last, pltpu.roll(g, tm - 1, 0))
        return g_prev * cw_ref[0:1, :] + g * cw_ref[1:2, :] + g_next * cw_ref[2:3, :]

    convs = [conv3(bi) for bi in range(nb)]

    for bi, c in [(bi, c) for bi in range(nb) for c in range(tm // Q_TILE)]:
        rows = slice(c * Q_TILE, (c + 1) * Q_TILE)
        conv = convs[bi]
        x = x_ref[bi, rows, :]
        hb = _modulated_norm(x, mod_ref, ng_ref).astype(BF16)
        token_major = lambda ref: ref[bi, c].astype(F32).T.astype(BF16)
        ys = [(oa_ref[bi, rows, 0:256].astype(F32) * conv[rows]).astype(BF16),
              token_major(yb_ref) * og_ref[bi, rows, 0:256],
              yc_ref[bi, rows, :] * og_ref[bi, rows, 256:512],
              token_major(yd_ref) * og_ref[bi, rows, 512:768]]
        merged = None
        for i in range(N_BRANCH):
            gate = jax.nn.sigmoid(_dot(hb, wg_ref[:, i * 1024:(i + 1) * 1024]))
            term = gate * _dot(ys[i], wbr_ref[i])
            merged = term if merged is None else merged + term
        out = _dot(merged.astype(BF16), wo_ref[...])
        xn = x + mod_ref[0, 2:3, :] * out
        if last:
            ms = jnp.mean(xn * xn, axis=-1, keepdims=True)
            xn = xn * lax.rsqrt(ms + EPS) * fg_ref[...]
        o_ref[bi, rows, :] = xn


def _merge(x, mod, ng, oa, og, yb, yc, yd, conv_w, wg, wbr, wo, fg, *, layer, nb, tm, last):
    b, t, d = x.shape
    nt = t // tm
    rb = tm // 8
    per_batch_mod = mod.shape[0] > 1
    assert b % nb == 0 and (nb == 1 or (nt == 1 and not per_batch_mod))
    tok = lambda w: pl.BlockSpec((nb, tm, w), lambda i, j: (i, j, 0))
    q_tiles = pl.BlockSpec((nb, tm // Q_TILE, 256, Q_TILE), lambda i, j: (i, j, 0, 0))
    const = lambda shape: pl.BlockSpec(shape, lambda i, j: (0,) * len(shape), pipeline_mode=pl.Buffered(1))
    in_specs = [tok(d),
                pl.BlockSpec((1, 3, d), (lambda i, j: (i, 0, 0)) if per_batch_mod else (lambda i, j: (0, 0, 0))),
                const((1, d)), tok(512),
                pl.BlockSpec((1, 8, 512), lambda i, j: (i * nb, jnp.maximum(j * rb - 1, 0), 0)),
                pl.BlockSpec((1, 8, 512), lambda i, j: (i * nb, jnp.minimum((j + 1) * rb, t // 8 - 1), 0)),
                tok(768), q_tiles, tok(256), q_tiles,
                _layer_spec(conv_w, layer), _layer_spec(wg, layer), _layer_spec(wbr, layer), _layer_spec(wo, layer),
                const((1, d))]
    return pl.pallas_call(
        functools.partial(_merge_kernel, tm=tm, last=last),
        out_shape=jax.ShapeDtypeStruct((b, t, d), F32),
        grid=(b // nb, nt),
        in_specs=in_specs,
        out_specs=tok(d),
        compiler_params=pltpu.CompilerParams(
            dimension_semantics=("parallel", "parallel"), vmem_limit_bytes=VMEM_LIMIT_BYTES),
        name="merge",
    )(x, mod, ng, oa, oa, oa, og, yb, yc, yd, conv_w, wg, wbr, wo, fg)


def _rope_tables(n_tokens, head_dim):
    rows = n_tokens // GRID_W
    row = np.repeat(np.arange(rows, dtype=np.float64), GRID_W)
    col = np.tile(np.arange(GRID_W, dtype=np.float64), rows)
    n_axis = head_dim // 4
    inv_freq = ROPE_THETA ** (-np.arange(n_axis, dtype=np.float64) / n_axis)
    ang = np.concatenate([row[:, None] * inv_freq, col[:, None] * inv_freq], axis=-1)
    cos, sin = np.cos(ang), np.sin(ang)
    reps = LANES // head_dim
    return (np.tile(np.concatenate([cos, cos], axis=-1), (1, reps)).astype(np.float32),
            np.tile(np.concatenate([-sin, sin], axis=-1), (1, reps)).astype(np.float32))


def _token_tiling(b, t, shared_mod):
    if t % TOKEN_TILE == 0:
        return 1, TOKEN_TILE
    assert t % Q_TILE == 0
    nb = max(1, TOKEN_TILE // t) if shared_mod else 1
    while b % nb:
        nb -= 1
    return nb, t


def kernel(x_prompt, x_sample, cache_gqa_k, cache_gqa_v, cache_diff_k, cache_diff_v, state_ret, c, c_ctx,
           w_ada, b_ada, norm_gain, w_in, conv_w, gqa_q_gain, gqa_k_gain, ret_decay, diff_lambda,
           diff_norm_gain, w_branch, w_mgate, w_out, final_gain):
    depth = w_in.shape[0]
    bc, tc, d = x_prompt.shape
    bl, tl, _ = x_sample.shape

    cond = jnp.concatenate([c_ctx[None], c], axis=0)
    n_cond = cond.shape[0]
    cond = jnp.pad(cond, ((0, -n_cond % 8), (0, 0)))
    mod = _adaln(cond, w_ada.astype(BF16), b_ada)
    mod = mod.reshape(depth, cond.shape[0], 3, d)

    rope = _rope_tables(tl, HEAD_W) + _rope_tables(tl, DIFF_HEAD_DIM)
    fg = final_gain.reshape(1, d)
    zero_state = jnp.zeros((bc, 2, RET_HEADS, HEAD_W, HEAD_W), F32)

    def to_heads(cache):
        b, p = cache.shape[:2]
        return jnp.transpose(cache.reshape(b, p, -1, HEAD_W), (0, 2, 1, 3)).astype(BF16)

    def to_features(cache):
        b, p = cache.shape[:2]
        return jnp.transpose(cache.reshape(b, p, -1), (0, 2, 1)).astype(BF16)

    w_all = w_in.astype(BF16)
    wg, wbr, wo = w_mgate.astype(BF16), w_branch.astype(BF16), w_out.astype(BF16)

    xp, xs = x_prompt, x_sample
    new_caches = ([], [], [], [], [])
    for l in range(depth):
        lambda_init = 0.8 - 0.6 * math.exp(-0.3 * l)
        ng = norm_gain[l].reshape(1, d)
        qg = jnp.tile(gqa_q_gain[l], 256 // HEAD_W).reshape(1, 256)
        kg = jnp.tile(gqa_k_gain[l], 128 // HEAD_W).reshape(1, 128)
        dgain = diff_norm_gain[l].reshape(HEAD_W, 1)

        for latent in (False, True):
            x = xs if latent else xp
            b, t, _ = x.shape
            m = mod[l, 1:1 + bl] if latent else mod[l, 0:1]
            nb, tm = _token_tiling(b, t, shared_mod=not latent)
            outs = _inproj(x, m, ng, w_all, qg, kg, rope, layer=l, latent=latent, nb=nb, tm=tm)
            oa, og, qb, kb, vb, qc, kc, vc, qd, kd, vd = outs[:11]
            kb, vb, kd, vd = [kb], [vb], [kd], [vd]
            if latent:
                kb.insert(0, to_heads(cache_gqa_k[:, l]))
                vb.insert(0, to_features(cache_gqa_v[:, l]))
                kd.insert(0, to_heads(cache_diff_k[:, l]))
                vd.insert(0, to_features(cache_diff_v[:, l]))
                s0 = state_ret[:, l]
            else:
                for lst, a in zip(new_caches[:4], outs[11:]):
                    lst.append(a)
                s0 = zero_state
            yb = _attention(qb, kb, vb, diff_lambda[l], dgain, diff=False, lambda_init=lambda_init)
            yd = _attention(qd, kd, vd, diff_lambda[l], dgain, diff=True, lambda_init=lambda_init)
            yc, s_new = _retention(qc, kc, vc, s0, ret_decay[l])
            if not latent:
                new_caches[4].append(s_new)
            x = _merge(x, m, ng, oa, og, yb, yc, yd, conv_w, wg, wbr, wo, fg, layer=l, nb=nb, tm=tm,
                       last=(l == depth - 1))
            if latent:
                xs = x
            else:
                xp = x

    ck = jnp.stack(new_caches[0], axis=1).reshape(bc, depth, tc, GQA_KV_HEADS, HEAD_W)
    cv = jnp.stack(new_caches[1], axis=1).reshape(bc, depth, tc, GQA_KV_HEADS, HEAD_W)
    cdk = jnp.stack(new_caches[2], axis=1).reshape(bc, depth, tc, DIFF_HEADS, 2, DIFF_HEAD_DIM)
    cdv = jnp.stack(new_caches[3], axis=1).reshape(bc, depth, tc, DIFF_HEADS, 2 * DIFF_HEAD_DIM)
    cs = jnp.stack(new_caches[4], axis=1)
    return (xp, xs, ck, cv, cdk, cdv, cs)
```

```python
import functools
import math

import jax
import jax.numpy as jnp
import numpy as np
from jax import lax
from jax.experimental import pallas as pl
from jax.experimental.pallas import tpu as pltpu

F32 = jnp.float32
BF16 = jnp.bfloat16

GRID_W = 64
BRANCH_W = 256
HEAD_W = 64
GQA_KV_HEADS = 2
RET_HEADS = 4
DIFF_HEADS = 4
DIFF_HEAD_DIM = 32
ROPE_THETA = 10000.0
EPS = 1e-6
LOG2_E = math.log2(math.e)
N_BRANCH = 4
LANES = 128
Q_TILE = 256
KEY_BLOCK = 256
TOKEN_TILE = 1024
VMEM_LIMIT_BYTES = 56 * 1024 * 1024

COL_A = 0
COL_B = 1024
COL_C = 1792
COL_D = 2816


def _vmem_bytes(shape, dtype):
    itemsize = jnp.dtype(dtype).itemsize
    sublanes = 8 * (4 // itemsize)
    rows = -(-shape[-2] // sublanes) * sublanes
    cols = -(-shape[-1] // LANES) * LANES
    return math.prod(shape[:-2]) * rows * cols * itemsize


def _silu(x):
    return x * jax.nn.sigmoid(x)


def _dot(a, b):
    return jnp.dot(a, b, preferred_element_type=F32)


def _group_ones(width):
    r = lax.broadcasted_iota(jnp.int32, (width, width), 0) >> 6
    c = lax.broadcasted_iota(jnp.int32, (width, width), 1) >> 6
    return jnp.where(r == c, 1.0, 0.0).astype(BF16)


def _head_rms(z, ones):
    ss = _dot((z * z).astype(BF16), ones)
    return z * lax.rsqrt(ss * (1.0 / HEAD_W) + EPS)


def _rope(x, cos_t, sin_t, half):
    lane = lax.broadcasted_iota(jnp.int32, (x.shape[0], LANES), 1)
    first = (lane & (2 * half - 1)) < half
    outs = []
    for j in range(x.shape[1] // LANES):
        xc = x[:, j * LANES:(j + 1) * LANES]
        partner = jnp.where(first, pltpu.roll(xc, LANES - half, 1), pltpu.roll(xc, half, 1))
        outs.append(xc * cos_t + partner * sin_t)
    return outs[0] if len(outs) == 1 else jnp.concatenate(outs, axis=1)


def _modulated_norm(x, mod_ref, ng_ref):
    ms = jnp.mean(x * x, axis=-1, keepdims=True)
    y = x * lax.rsqrt(ms + EPS) * ng_ref[...]
    return y * (1.0 + mod_ref[0, 1:2, :]) + mod_ref[0, 0:1, :]


def _adaln_kernel(cond_ref, w_ref, b_ref, o_ref):
    a = _silu(cond_ref[...]).astype(BF16)
    o_ref[0] = _dot(a, w_ref[0]) + b_ref[0]


def _adaln(cond, w_ada, b_ada):
    depth, d, d3 = w_ada.shape
    rows = cond.shape[0]
    return pl.pallas_call(
        _adaln_kernel,
        out_shape=jax.ShapeDtypeStruct((depth, rows, d3), F32),
        grid=(depth,),
        in_specs=[pl.BlockSpec((rows, d), lambda l: (0, 0)),
                  pl.BlockSpec((1, d, d3), lambda l: (l, 0, 0)),
                  pl.BlockSpec((1, 1, d3), lambda l: (l, 0, 0))],
        out_specs=pl.BlockSpec((1, rows, d3), lambda l: (l, 0, 0)),
        compiler_params=pltpu.CompilerParams(vmem_limit_bytes=VMEM_LIMIT_BYTES),
        name="adaln",
    )(cond, w_ada, b_ada.reshape(depth, 1, d3))


def _inproj_kernel(*refs, latent):
    if latent:
        (x_ref, mod_ref, ng_ref, w_ref, qg_ref, kg_ref, c64_ref, s64_ref, c32_ref, s32_ref,
         oa_ref, og_ref, qb_ref, kb_ref, vb_ref, qc_ref, kc_ref, vc_ref, qd_ref, kd_ref, vd_ref) = refs
    else:
        (x_ref, mod_ref, ng_ref, w_ref, qg_ref, kg_ref,
         oa_ref, og_ref, qb_ref, kb_ref, vb_ref, qc_ref, kc_ref, vc_ref, qd_ref, kd_ref, vd_ref,
         ck_ref, cv_ref, cdk_ref, cdv_ref) = refs
    nb, tm = x_ref.shape[:2]
    for bi, c in [(bi, c) for bi in range(nb) for c in range(tm // Q_TILE)]:
        rows = slice(c * Q_TILE, (c + 1) * Q_TILE)
        hb = _modulated_norm(x_ref[bi, rows, :], mod_ref, ng_ref).astype(BF16)

        z = _dot(hb, w_ref[:, COL_A:COL_A + 1024])
        oa_ref[bi, rows, 0:256] = (z[:, 0:256] * _silu(z[:, 768:1024])).astype(BF16)
        oa_ref[bi, rows, 256:512] = (z[:, 256:512] * z[:, 512:768]).astype(BF16)

        z = _dot(hb, w_ref[:, COL_B:COL_B + 768])
        q = _head_rms(z[:, 0:256], _group_ones(256)) * qg_ref[...]
        k = _head_rms(z[:, 256:384], _group_ones(128)) * kg_ref[...]
        v = z[:, 384:512]
        if latent:
            q = _rope(q, c64_ref[rows, :], s64_ref[rows, :], 32)
            k_att = _rope(k, c64_ref[rows, :], s64_ref[rows, :], 32)
        else:
            k_att = k
            ck_ref[bi, rows, :] = k
            cv_ref[bi, rows, :] = v
        qb_ref[bi, c] = (q * (HEAD_W ** -0.5 * LOG2_E)).T.astype(BF16)
        vb_ref[bi, :, rows] = v.T.astype(BF16)
        for h in range(GQA_KV_HEADS):
            kb_ref[bi, h, rows, :] = k_att[:, h * HEAD_W:(h + 1) * HEAD_W].astype(BF16)
        og_ref[bi, rows, 0:256] = _silu(z[:, 512:768]).astype(BF16)

        z = _dot(hb, w_ref[:, COL_C:COL_C + 1024])
        qc_ref[bi, rows, :] = z[:, 0:256].astype(BF16)
        kc_ref[bi, c] = (z[:, 256:512] * (HEAD_W ** -0.5)).T.astype(BF16)
        vc_ref[bi, rows, :] = z[:, 512:768].astype(BF16)
        og_ref[bi, rows, 256:512] = _silu(z[:, 768:1024]).astype(BF16)

        z = _dot(hb, w_ref[:, COL_D:COL_D + 1024])
        q = z[:, 0:256]
        k = z[:, 256:512]
        v = z[:, 512:768]
        if latent:
            q = _rope(q, c32_ref[rows, :], s32_ref[rows, :], 16)
            k_att = _rope(k, c32_ref[rows, :], s32_ref[rows, :], 16)
        else:
            k_att = k
            cdk_ref[bi, rows, :] = k
            cdv_ref[bi, rows, :] = v
        qt = (q * (DIFF_HEAD_DIM ** -0.5 * LOG2_E)).T.astype(BF16)
        dd = DIFF_HEAD_DIM
        zeros = jnp.zeros((2 * dd, Q_TILE), BF16)
        for h in range(DIFF_HEADS):
            src, dst = h * HEAD_W, h * LANES
            qd_ref[bi, c, dst:dst + dd, :] = qt[src:src + dd]
            qd_ref[bi, c, dst + dd:dst + 3 * dd, :] = zeros
            qd_ref[bi, c, dst + 3 * dd:dst + 4 * dd, :] = qt[src + dd:src + 2 * dd]
        vd_ref[bi, :, rows] = v.T.astype(BF16)
        for h in range(DIFF_HEADS):
            kd_ref[bi, h, rows, :] = k_att[:, h * HEAD_W:(h + 1) * HEAD_W].astype(BF16)
        og_ref[bi, rows, 512:768] = _silu(z[:, 768:1024]).astype(BF16)


def _layer_spec(a, layer):
    return pl.BlockSpec((None,) + a.shape[1:], lambda i, j: (layer,) + (0,) * (a.ndim - 1),
                        pipeline_mode=pl.Buffered(1))


def _inproj(x, mod, ng, w, qg, kg, rope_tabs, *, layer, latent, nb, tm):
    b, t, d = x.shape
    nt = t // tm
    per_batch_mod = mod.shape[0] > 1
    assert b % nb == 0 and not (per_batch_mod and nb > 1)
    tok = lambda w: pl.BlockSpec((nb, tm, w), lambda i, j: (i, j, 0))
    tok_t = lambda w: pl.BlockSpec((nb, w, tm), lambda i, j: (i, 0, j))
    heads = lambda n: pl.BlockSpec((nb, n, tm, HEAD_W), lambda i, j: (i, 0, j, 0))
    const = lambda shape: pl.BlockSpec(shape, lambda i, j: (0,) * len(shape), pipeline_mode=pl.Buffered(1))
    in_specs = [tok(d),
                pl.BlockSpec((1, 3, d), (lambda i, j: (i, 0, 0)) if per_batch_mod else (lambda i, j: (0, 0, 0))),
                const((1, d)), _layer_spec(w, layer), const((1, 256)), const((1, 128))]
    args = [x, mod, ng, w, qg, kg]
    if latent:
        in_specs += [pl.BlockSpec((tm, LANES), lambda i, j: (j, 0))] * 4
        args += list(rope_tabs)
    sd = jax.ShapeDtypeStruct
    qn = tm // Q_TILE
    q_tiles = lambda w: pl.BlockSpec((nb, qn, w, Q_TILE), lambda i, j: (i, j, 0, 0))
    out_shape = [sd((b, t, 512), BF16), sd((b, t, 768), BF16),
                 sd((b, t // Q_TILE, 256, Q_TILE), BF16), sd((b, GQA_KV_HEADS, t, HEAD_W), BF16), sd((b, 128, t), BF16),
                 sd((b, t, 256), BF16), sd((b, t // Q_TILE, 256, Q_TILE), BF16), sd((b, t, 256), BF16),
                 sd((b, t // Q_TILE, 512, Q_TILE), BF16), sd((b, DIFF_HEADS, t, HEAD_W), BF16), sd((b, 256, t), BF16)]
    out_specs = [tok(512), tok(768), q_tiles(256), heads(GQA_KV_HEADS), tok_t(128),
                 tok(256), q_tiles(256), tok(256), q_tiles(512), heads(DIFF_HEADS), tok_t(256)]
    if not latent:
        out_shape += [sd((b, t, 128), F32), sd((b, t, 128), F32), sd((b, t, 256), F32), sd((b, t, 256), F32)]
        out_specs += [tok(128), tok(128), tok(256), tok(256)]
    return pl.pallas_call(
        functools.partial(_inproj_kernel, latent=latent),
        out_shape=out_shape,
        grid=(b // nb, nt),
        in_specs=in_specs,
        out_specs=out_specs,
        compiler_params=pltpu.CompilerParams(
            dimension_semantics=("parallel", "parallel"), vmem_limit_bytes=VMEM_LIMIT_BYTES),
        name="inproj_lat" if latent else "inproj_ctx",
    )(*args)


def _attn_kernel(*refs, n_kv, n_q, kb, has_cache, diff, lambda_init):
    if has_cache:
        qt_ref, kc_ref, vc_ref, kn_ref, vn_ref, lam_ref, gain_ref, o_ref, s_even, s_odd = refs
        sources = [(kc_ref, vc_ref), (kn_ref, vn_ref)]
    else:
        qt_ref, kn_ref, vn_ref, lam_ref, gain_ref, o_ref, s_even, s_odd = refs
        sources = [(kn_ref, vn_ref)]
    blocks = [(k_ref, v_ref, a) for k_ref, v_ref in sources for a in range(0, k_ref.shape[2], kb)]
    lanes = 2 * Q_TILE
    ones = jnp.ones((16, kb), BF16)
    if diff:
        lp = lam_ref[...]
        lam = (jnp.exp(jnp.sum(lp[0:1] * lp[1:2], axis=-1, keepdims=True))
               - jnp.exp(jnp.sum(lp[2:3] * lp[3:4], axis=-1, keepdims=True)) + lambda_init)

    def stage(i_new, s_new, i_old, s_old, m_old):
        if i_new is not None:
            h_new, u_new = i_new // n_q, i_new % n_q
            q2 = qt_ref[0, u_new, pl.ds(pl.multiple_of(h_new * LANES, LANES), LANES), :]
            qt = jnp.concatenate([q2[:HEAD_W], q2[HEAD_W:]], axis=1)
        if i_old is not None:
            h_old, u_old = i_old // n_q, i_old % n_q
            v_rows = pl.ds(pl.multiple_of(h_old * HEAD_W, HEAD_W), HEAD_W)
            m8_old = jnp.broadcast_to(m_old, (8, lanes))
        m8, acc = None, None
        for j, (k_ref, v_ref, a) in enumerate(blocks):
            if i_new is not None:
                s = _dot(k_ref[0, h_new, a:a + kb, :], qt)
                s_new[j * kb:(j + 1) * kb, :] = s
                mj = jnp.max(s.reshape(kb // 8, 8, lanes), axis=0)
                m8 = mj if m8 is None else jnp.maximum(m8, mj)
            if i_old is not None:
                s = s_old[j * kb:(j + 1) * kb, :]
                p = jnp.exp2(s.reshape(kb // 8, 8, lanes) - m8_old[None]).reshape(kb, lanes).astype(BF16)
                vt = jnp.concatenate([v_ref[0, v_rows, a:a + kb], ones], axis=0)
                part = _dot(vt, p)
                acc = part if acc is None else acc + part
        if i_old is not None:
            o = acc[:HEAD_W] / acc[HEAD_W:HEAD_W + 1]
            if diff:
                od = o[:, :Q_TILE] - lam * o[:, Q_TILE:]
                ms = jnp.mean(od * od, axis=0, keepdims=True)
                y = od * lax.rsqrt(ms + EPS) * gain_ref[...] * (1.0 - lambda_init)
                o_ref[0, u_old, pl.ds(pl.multiple_of(h_old * HEAD_W, HEAD_W), HEAD_W), :] = y.astype(o_ref.dtype)
            else:
                y = jnp.concatenate([o[:, :Q_TILE], o[:, Q_TILE:]], axis=0)
                o_ref[0, u_old, pl.ds(pl.multiple_of(h_old * LANES, LANES), LANES), :] = y.astype(o_ref.dtype)
        return None if i_new is None else jnp.max(m8, axis=0, keepdims=True)

    n_units = n_kv * n_q
    assert n_units % 2 == 0

    def two_stages(t, m_even):
        m_odd = stage(2 * t + 1, s_odd, 2 * t, s_even, m_even)
        return stage(2 * t + 2, s_even, 2 * t + 1, s_odd, m_odd)

    n_pairs = n_units // 2 - 1
    m_even = stage(0, s_even, None, None, None)
    if n_pairs % 2:
        m_even = two_stages(0, m_even)
    m_even = lax.fori_loop(n_pairs % 2, n_pairs, two_stages, m_even, unroll=2)
    m_odd = stage(n_units - 1, s_odd, n_units - 2, s_even, m_even)
    stage(None, None, n_units - 1, s_odd, m_odd)


def _attention(qt, k_parts, vt_parts, lam_p, gain, *, diff, lambda_init):
    b, n_q, qrows, _ = qt.shape
    n_kv = k_parts[0].shape[1]
    lens = [k.shape[2] for k in k_parts]
    assert qrows == n_kv * LANES and all(v.shape[1] == n_kv * HEAD_W for v in vt_parts)
    kb = KEY_BLOCK if all(n % KEY_BLOCK == 0 for n in lens) else math.gcd(*lens)
    kern = functools.partial(_attn_kernel, n_kv=n_kv, n_q=n_q, kb=kb, has_cache=len(k_parts) == 2, diff=diff,
                             lambda_init=lambda_init)
    kv = [a for pair in zip(k_parts, vt_parts) for a in pair]
    score_bytes = 2 * sum(lens) * 2 * Q_TILE * 4
    out_bytes = 2 * _vmem_bytes((n_q, 256, Q_TILE), BF16)
    operand_bytes = sum(_vmem_bytes(a.shape[1:], a.dtype) for a in [qt] + kv)
    double_buffered = score_bytes + out_bytes + 2 * operand_bytes <= VMEM_LIMIT_BYTES
    mode = {} if double_buffered else dict(pipeline_mode=pl.Buffered(1))
    whole = lambda a: pl.BlockSpec((1,) + a.shape[1:], lambda i: (i,) + (0,) * (a.ndim - 1), **mode)
    return pl.pallas_call(
        kern,
        out_shape=jax.ShapeDtypeStruct((b, n_q, 256, Q_TILE), BF16),
        grid=(b,),
        in_specs=[whole(qt)] + [whole(a) for a in kv] + [pl.BlockSpec(lam_p.shape, lambda i: (0, 0)),
                                                          pl.BlockSpec(gain.shape, lambda i: (0, 0))],
        out_specs=pl.BlockSpec((1, n_q, 256, Q_TILE), lambda i: (i, 0, 0, 0)),
        scratch_shapes=[pltpu.VMEM((sum(lens), 2 * Q_TILE), F32), pltpu.VMEM((sum(lens), 2 * Q_TILE), F32)],
        compiler_params=pltpu.CompilerParams(
            dimension_semantics=("parallel",), vmem_limit_bytes=VMEM_LIMIT_BYTES),
        name="diff_attn" if diff else "gqa_attn",
    )(qt, *kv, lam_p, gain)


def _retention_kernel(q_ref, kt_ref, v_ref, s0_ref, dec_ref, o_ref, sn_ref, kv_scr, sf_scr, sb_scr,
                      mask_scr, qd_scr, kd_scr, cd_scr, *, n_chunks):
    c_len = Q_TILE

    @pl.when(pl.program_id(0) == 0)
    def _fill_decay_tables():
        dec = dec_ref[...]
        lg = jnp.minimum(dec, 0.0) - jnp.log1p(jnp.exp(-jnp.abs(dec)))
        row = lax.broadcasted_iota(jnp.int32, (c_len, c_len), 0)
        col = lax.broadcasted_iota(jnp.int32, (c_len, c_len), 1)
        rel = (row - col).astype(F32)
        pos = lax.broadcasted_iota(jnp.int32, (c_len, HEAD_W), 0).astype(F32)
        pos_t = lax.broadcasted_iota(jnp.int32, (HEAD_W, c_len), 1).astype(F32)
        for h in range(RET_HEADS):
            lf = lg[0:1, h:h + 1]
            lb = lg[1:2, h:h + 1]
            mask_scr[h] = (jnp.where(rel >= 0, jnp.exp(lf * jnp.maximum(rel, 0.0)), 0.0)
                           + jnp.where(rel <= 0, jnp.exp(lb * jnp.maximum(-rel, 0.0)), 0.0))
            qd_scr[h, 0] = jnp.exp(lf * (pos + 1.0))
            qd_scr[h, 1] = jnp.exp(lb * (c_len - pos))
            kd_scr[h, 0] = jnp.exp(lf * (c_len - 1.0 - pos_t))
            kd_scr[h, 1] = jnp.exp(lb * pos_t)
            cd_scr[h, 0] = jnp.broadcast_to(jnp.exp(lf * c_len), (HEAD_W, HEAD_W))
            cd_scr[h, 1] = jnp.broadcast_to(jnp.exp(lb * c_len), (HEAD_W, HEAD_W))

    def chunk_of(ref, c, h):
        start = pl.multiple_of(c * c_len, c_len)
        return ref[0, pl.ds(start, c_len), h * HEAD_W:(h + 1) * HEAD_W]

    unroll = 2 if n_chunks % 2 == 0 else 1

    def kv_body(c, carry):
        for h in range(RET_HEADS):
            kt = kt_ref[0, c, h * HEAD_W:(h + 1) * HEAD_W, :].astype(F32)
            vc = chunk_of(v_ref, c, h)
            kv_scr[c, h, 0] = _dot((kt * kd_scr[h, 0]).astype(BF16), vc)
            kv_scr[c, h, 1] = _dot((kt * kd_scr[h, 1]).astype(BF16), vc)
        return carry

    lax.fori_loop(0, n_chunks, kv_body, 0, unroll=unroll)

    def scan_body(i, states):
        sf, sb = states
        cb = n_chunks - 1 - i
        new_f, new_b = [], []
        for h in range(RET_HEADS):
            sf_scr[i, h] = sf[h]
            sb_scr[cb, h] = sb[h]
            new_f.append(sf[h] * cd_scr[h, 0] + kv_scr[i, h, 0])
            new_b.append(sb[h] * cd_scr[h, 1] + kv_scr[cb, h, 1])
        return tuple(new_f), tuple(new_b)

    sf_fin, sb_fin = lax.fori_loop(0, n_chunks, scan_body,
                                   (tuple(s0_ref[0, 0, h] for h in range(RET_HEADS)),
                                    tuple(s0_ref[0, 1, h] for h in range(RET_HEADS))))
    for h in range(RET_HEADS):
        sn_ref[0, 0, h] = sf_fin[h]
        sn_ref[0, 1, h] = sb_fin[h]

    def out_body(c, carry):
        outs = []
        for h in range(RET_HEADS):
            qc, vc = chunk_of(q_ref, c, h), chunk_of(v_ref, c, h)
            a = _dot(qc, kt_ref[0, c, h * HEAD_W:(h + 1) * HEAD_W, :]) * mask_scr[h]
            o = _dot(a.astype(BF16), vc)
            o = o + _dot(qc, sf_scr[c, h].astype(BF16)) * qd_scr[h, 0]
            o = o + _dot(qc, sb_scr[c, h].astype(BF16)) * qd_scr[h, 1]
            ms = jnp.mean(o * o, axis=-1, keepdims=True)
            outs.append(o * lax.rsqrt(ms + EPS))
        start = pl.multiple_of(c * c_len, c_len)
        o_ref[0, pl.ds(start, c_len), :] = jnp.concatenate(outs, axis=-1).astype(o_ref.dtype)
        return carry

    lax.fori_loop(0, n_chunks, out_body, 0, unroll=unroll)


def _retention(q, kt, v, s0, decay):
    b, t, w = q.shape
    n_chunks = kt.shape[1]
    assert n_chunks * Q_TILE == t
    tok = pl.BlockSpec((1, t, w), lambda i: (i, 0, 0))
    st = pl.BlockSpec((1, 2, RET_HEADS, HEAD_W, HEAD_W), lambda i: (i, 0, 0, 0, 0))
    state_scr = pltpu.VMEM((n_chunks, RET_HEADS, HEAD_W, HEAD_W), F32)
    return pl.pallas_call(
        functools.partial(_retention_kernel, n_chunks=n_chunks),
        out_shape=[jax.ShapeDtypeStruct((b, t, w), BF16),
                   jax.ShapeDtypeStruct((b, 2, RET_HEADS, HEAD_W, HEAD_W), F32)],
        grid=(b,),
        in_specs=[tok, pl.BlockSpec((1, n_chunks, w, Q_TILE), lambda i: (i, 0, 0, 0)), tok, st,
                  pl.BlockSpec((2, RET_HEADS), lambda i: (0, 0))],
        out_specs=[tok, st],
        scratch_shapes=[pltpu.VMEM((n_chunks, RET_HEADS, 2, HEAD_W, HEAD_W), F32), state_scr, state_scr,
                        pltpu.VMEM((RET_HEADS, Q_TILE, Q_TILE), F32),
                        pltpu.VMEM((RET_HEADS, 2, Q_TILE, HEAD_W), F32),
                        pltpu.VMEM((RET_HEADS, 2, HEAD_W, Q_TILE), F32),
                        pltpu.VMEM((RET_HEADS, 2, HEAD_W, HEAD_W), F32)],
        compiler_params=pltpu.CompilerParams(
            dimension_semantics=("arbitrary",), vmem_limit_bytes=VMEM_LIMIT_BYTES),
        name="retention",
    )(q, kt, v, s0, decay)


def _merge_kernel(x_ref, mod_ref, ng_ref, oa_ref, prev_ref, next_ref, og_ref, yb_ref, yc_ref, yd_ref,
                  cw_ref, wg_ref, wbr_ref, wo_ref, fg_ref, o_ref, *, tm, last):
    j = pl.program_id(1)
    nb = x_ref.shape[0]

    def conv3(bi):
        g = oa_ref[bi, :, 256:512].astype(F32)
        row = lax.broadcasted_iota(jnp.int32, g.shape, 0)
        g_first = jnp.where(j == 0, 0.0, prev_ref[0, 7:8, 256:512].astype(F32))
        g_last = jnp.where(j == pl.num_programs(1) - 1, 0.0, next_ref[0, 0:1, 256:512].astype(F32))
        g_prev = jnp.where(row == 0, g_first, pltpu.roll(g, 1, 0))
        g_next = jnp.where(row == tm - 1, g_last, pltpu.roll(g, tm - 1, 0))
        return g_prev * cw_ref[0:1, :] + g * cw_ref[1:2, :] + g_next * cw_ref[2:3, :]

    convs = [conv3(bi) for bi in range(nb)]

    for bi, c in [(bi, c) for bi in range(nb) for c in range(tm // Q_TILE)]:
        rows = slice(c * Q_TILE, (c + 1) * Q_TILE)
        conv = convs[bi]
        x = x_ref[bi, rows, :]
        hb = _modulated_norm(x, mod_ref, ng_ref).astype(BF16)
        token_major = lambda ref: ref[bi, c].astype(F32).T.astype(BF16)
        ys = [(oa_ref[bi, rows, 0:256].astype(F32) * conv[rows]).astype(BF16),
              token_major(yb_ref) * og_ref[bi, rows, 0:256],
              yc_ref[bi, rows, :] * og_ref[bi, rows, 256:512],
              token_major(yd_ref) * og_ref[bi, rows, 512:768]]
        merged = None
        for i in range(N_BRANCH):
            gate = jax.nn.sigmoid(_dot(hb, wg_ref[:, i * 1024:(i + 1) * 1024]))
            term = gate * _dot(ys[i], wbr_ref[i])
            merged = term if merged is None else merged + term
        out = _dot(merged.astype(BF16), wo_ref[...])
        xn = x + mod_ref[0, 2:3, :] * out
        if last:
            ms = jnp.mean(xn * xn, axis=-1, keepdims=True)
            xn = xn * lax.rsqrt(ms + EPS) * fg_ref[...]
        o_ref[bi, rows, :] = xn


def _merge(x, mod, ng, oa, og, yb, yc, yd, conv_w, wg, wbr, wo, fg, *, layer, nb, tm, last):
    b, t, d = x.shape
    nt = t // tm
    rb = tm // 8
    per_batch_mod = mod.shape[0] > 1
    assert b % nb == 0 and (nb == 1 or (nt == 1 and not per_batch_mod))
    tok = lambda w: pl.BlockSpec((nb, tm, w), lambda i, j: (i, j, 0))
    q_tiles = pl.BlockSpec((nb, tm // Q_TILE, 256, Q_TILE), lambda i, j: (i, j, 0, 0))
    const = lambda shape: pl.BlockSpec(shape, lambda i, j: (0,) * len(shape), pipeline_mode=pl.Buffered(1))
    in_specs = [tok(d),
                pl.BlockSpec((1, 3, d), (lambda i, j: (i, 0, 0)) if per_batch_mod else (lambda i, j: (0, 0, 0))),
                const((1, d)), tok(512),
                pl.BlockSpec((1, 8, 512), lambda i, j: (i * nb, jnp.maximum(j * rb - 1, 0), 0)),
                pl.BlockSpec((1, 8, 512), lambda i, j: (i * nb, jnp.minimum((j + 1) * rb, t // 8 - 1), 0)),
                tok(768), q_tiles, tok(256), q_tiles,
                _layer_spec(conv_w, layer), _layer_spec(wg, layer), _layer_spec(wbr, layer), _layer_spec(wo, layer),
                const((1, d))]
    return pl.pallas_call(
        functools.partial(_merge_kernel, tm=tm, last=last),
        out_shape=jax.ShapeDtypeStruct((b, t, d), F32),
        grid=(b // nb, nt),
        in_specs=in_specs,
        out_specs=tok(d),
        compiler_params=pltpu.CompilerParams(
            dimension_semantics=("parallel", "parallel"), vmem_limit_bytes=VMEM_LIMIT_BYTES),
        name="merge",
    )(x, mod, ng, oa, oa, oa, og, yb, yc, yd, conv_w, wg, wbr, wo, fg)


def _rope_tables(n_tokens, head_dim):
    rows = n_tokens // GRID_W
    row = np.repeat(np.arange(rows, dtype=np.float64), GRID_W)
    col = np.tile(np.arange(GRID_W, dtype=np.float64), rows)
    n_axis = head_dim // 4
    inv_freq = ROPE_THETA ** (-np.arange(n_axis, dtype=np.float64) / n_axis)
    ang = np.concatenate([row[:, None] * inv_freq, col[:, None] * inv_freq], axis=-1)
    cos, sin = np.cos(ang), np.sin(ang)
    reps = LANES // head_dim
    return (np.tile(np.concatenate([cos, cos], axis=-1), (1, reps)).astype(np.float32),
            np.tile(np.concatenate([-sin, sin], axis=-1), (1, reps)).astype(np.float32))


def _token_tiling(b, t, shared_mod):
    if t % TOKEN_TILE == 0:
        return 1, TOKEN_TILE
    assert t % Q_TILE == 0
    nb = max(1, TOKEN_TILE // t) if shared_mod else 1
    while b % nb:
        nb -= 1
    return nb, t


def kernel(x_prompt, x_sample, cache_gqa_k, cache_gqa_v, cache_diff_k, cache_diff_v, state_ret, c, c_ctx,
           w_ada, b_ada, norm_gain, w_in, conv_w, gqa_q_gain, gqa_k_gain, ret_decay, diff_lambda,
           diff_norm_gain, w_branch, w_mgate, w_out, final_gain):
    depth = w_in.shape[0]
    bc, tc, d = x_prompt.shape
    bl, tl, _ = x_sample.shape

    cond = jnp.concatenate([c_ctx[None], c], axis=0)
    n_cond = cond.shape[0]
    cond = jnp.pad(cond, ((0, -n_cond % 8), (0, 0)))
    mod = _adaln(cond, w_ada.astype(BF16), b_ada)
    mod = mod.reshape(depth, cond.shape[0], 3, d)

    rope = _rope_tables(tl, HEAD_W) + _rope_tables(tl, DIFF_HEAD_DIM)
    fg = final_gain.reshape(1, d)
    zero_state = jnp.zeros((bc, 2, RET_HEADS, HEAD_W, HEAD_W), F32)

    def to_heads(cache):
        b, p = cache.shape[:2]
        return jnp.transpose(cache.reshape(b, p, -1, HEAD_W), (0, 2, 1, 3)).astype(BF16)

    def to_features(cache):
        b, p = cache.shape[:2]
        return jnp.transpose(cache.reshape(b, p, -1), (0, 2, 1)).astype(BF16)

    w_all = w_in.astype(BF16)
    wg, wbr, wo = w_mgate.astype(BF16), w_branch.astype(BF16), w_out.astype(BF16)

    xp, xs = x_prompt, x_sample
    new_caches = ([], [], [], [], [])
    for l in range(depth):
        lambda_init = 0.8 - 0.6 * math.exp(-0.3 * l)
        ng = norm_gain[l].reshape(1, d)
        qg = jnp.tile(gqa_q_gain[l], 256 // HEAD_W).reshape(1, 256)
        kg = jnp.tile(gqa_k_gain[l], 128 // HEAD_W).reshape(1, 128)
        dgain = diff_norm_gain[l].reshape(HEAD_W, 1)

        for latent in (False, True):
            x = xs if latent else xp
            b, t, _ = x.shape
            m = mod[l, 1:1 + bl] if latent else mod[l, 0:1]
            nb, tm = _token_tiling(b, t, shared_mod=not latent)
            outs = _inproj(x, m, ng, w_all, qg, kg, rope, layer=l, latent=latent, nb=nb, tm=tm)
            oa, og, qb, kb, vb, qc, kc, vc, qd, kd, vd = outs[:11]
            kb, vb, kd, vd = [kb], [vb], [kd], [vd]
            if latent:
                kb.insert(0, to_heads(cache_gqa_k[:, l]))
                vb.insert(0, to_features(cache_gqa_v[:, l]))
                kd.insert(0, to_heads(cache_diff_k[:, l]))
                vd.insert(0, to_features(cache_diff_v[:, l]))
                s0 = state_ret[:, l]
            else:
                for lst, a in zip(new_caches[:4], outs[11:]):
                    lst.append(a)
                s0 = zero_state
            yb = _attention(qb, kb, vb, diff_lambda[l], dgain, diff=False, lambda_init=lambda_init)
            yd = _attention(qd, kd, vd, diff_lambda[l], dgain, diff=True, lambda_init=lambda_init)
            yc, s_new = _retention(qc, kc, vc, s0, ret_decay[l])
            if not latent:
                new_caches[4].append(s_new)
            x = _merge(x, m, ng, oa, og, yb, yc, yd, conv_w, wg, wbr, wo, fg, layer=l, nb=nb, tm=tm,
                       last=(l == depth - 1))
            if latent:
                xs = x
            else:
                xp = x

    ck = jnp.stack(new_caches[0], axis=1).reshape(bc, depth, tc, GQA_KV_HEADS, HEAD_W)
    cv = jnp.stack(new_caches[1], axis=1).reshape(bc, depth, tc, GQA_KV_HEADS, HEAD_W)
    cdk = jnp.stack(new_caches[2], axis=1).reshape(bc, depth, tc, DIFF_HEADS, 2, DIFF_HEAD_DIM)
    cdv = jnp.stack(new_caches[3], axis=1).reshape(bc, depth, tc, DIFF_HEADS, 2 * DIFF_HEAD_DIM)
    cs = jnp.stack(new_caches[4], axis=1)
    return (xp, xs, ck, cv, cdk, cdv, cs)
```
